```python
import math
import jax, jax.numpy as jnp
from jax import lax
import numpy as np

D_MODEL = 1024
BATCH = 8
SEQ = 4096
DEPTH = 1

N_MEM = 256
D_FF = 2816
GM_WIDTH = 512
GM_GROUPS = 4
GM_GROUP_DIM = GM_WIDTH // GM_GROUPS
GM_CHUNK = 128
DIFF_HEADS = 4
DIFF_HEAD_DIM = 64
DIFF_V_DIM = 2 * DIFF_HEAD_DIM
DIFF_Q_WIDTH = DIFF_HEADS * 2 * DIFF_HEAD_DIM
DIFF_V_WIDTH = DIFF_HEADS * DIFF_V_DIM
Q_BLOCK = 128
MEM_HEADS = 4
MEM_HEAD_DIM = 64
MEM_WIDTH = MEM_HEADS * MEM_HEAD_DIM
N_BRANCH = 3
ROPE_THETA = 500000.0
ROPE_DIM = DIFF_HEAD_DIM // 4
DEEPNORM_ALPHA = (2 * DEPTH) ** 0.25
DEEPNORM_BETA = (8 * DEPTH) ** -0.25
LN_EPS = 1e-5
IN_SPLITS = [GM_WIDTH, GM_WIDTH, DIFF_Q_WIDTH, DIFF_Q_WIDTH, DIFF_V_WIDTH, MEM_WIDTH]
IN_WIDTH = sum(IN_SPLITS) + N_BRANCH * D_MODEL

kernel_name = "hybrid_gmlp_diffattn_memxattn_macaron_deepnorm"


def layer_norm(x, g, b):
    xf = x.astype(jnp.float32)
    mu = jnp.mean(xf, axis=-1, keepdims=True)
    var = jnp.mean(jnp.square(xf - mu), axis=-1, keepdims=True)
    y = (xf - mu) * lax.rsqrt(var + LN_EPS) * g.astype(jnp.float32) + b.astype(jnp.float32)
    return y.astype(x.dtype)


def rms_norm(x, g):
    xf = x.astype(jnp.float32)
    y = xf * lax.rsqrt(jnp.mean(jnp.square(xf), axis=-1, keepdims=True) + LN_EPS) * g.astype(jnp.float32)
    return y.astype(x.dtype)


def swiglu(x, w_in, w_out):
    gate, up = jnp.split(x @ w_in, 2, axis=-1)
    return (jax.nn.silu(gate) * up) @ w_out


def rope_tables(positions):
    inv_freq = ROPE_THETA ** (-jnp.arange(0, ROPE_DIM, 2, dtype=jnp.float32) / ROPE_DIM)
    ang = positions.astype(jnp.float32)[..., None] * inv_freq
    return jnp.cos(ang), jnp.sin(ang)


def apply_partial_rope(t, cos, sin):
    c = cos[:, :, None, None, :]
    s = sin[:, :, None, None, :]
    half = ROPE_DIM // 2
    rot = t[..., :ROPE_DIM].astype(jnp.float32)
    x1, x2 = rot[..., :half], rot[..., half:]
    rotated = jnp.concatenate([x1 * c - x2 * s, x2 * c + x1 * s], axis=-1).astype(t.dtype)
    return jnp.concatenate([rotated, t[..., ROPE_DIM:]], axis=-1)


def gmlp_branch(u, v, ln_g, ln_b, w_s, b_s):
    B, S, _ = v.shape
    nc = S // GM_CHUNK
    vn = layer_norm(v, ln_g, ln_b).reshape(B, nc, GM_CHUNK, GM_GROUPS, GM_GROUP_DIM)
    causal = jnp.tril(jnp.ones((GM_CHUNK, GM_CHUNK), dtype=w_s.dtype))
    mixed = jnp.einsum('gts,bcsgd->bctgd', w_s * causal, vn) + b_s.T[:, :, None]
    return u * mixed.reshape(B, S, GM_WIDTH)


def diff_attention(q, k, v, lam, lam_init, norm_g):
    B, S = q.shape[0], q.shape[1]
    nb = S // Q_BLOCK
    scale = DIFF_HEAD_DIM ** -0.5
    q_blocks = jnp.moveaxis(q.reshape(B, nb, Q_BLOCK, DIFF_HEADS, 2, DIFF_HEAD_DIM), 1, 0)
    kpos = jnp.arange(S)

    def one_block(args):
        q_blk, bi = args
        s = jnp.einsum('bqhrd,bkhrd->bhrqk', q_blk, k).astype(jnp.float32) * scale
        qpos = bi * Q_BLOCK + jnp.arange(Q_BLOCK)
        s = jnp.where(kpos[None, :] <= qpos[:, None], s, -jnp.inf)
        p = jax.nn.softmax(s, axis=-1)
        a = p[:, :, 0] - lam * p[:, :, 1]
        return jnp.einsum('bhqk,bkhe->bqhe', a.astype(v.dtype), v)

    o = lax.map(one_block, (q_blocks, jnp.arange(nb)))
    o = jnp.moveaxis(o, 0, 1).reshape(B, S, DIFF_HEADS, DIFF_V_DIM)
    o = rms_norm(o, norm_g) * (1.0 - lam_init)
    return o.reshape(B, S, DIFF_V_WIDTH)


def memory_attention(q, mem, w_kv):
    B, M = mem.shape[0], mem.shape[1]
    k, v = jnp.split(mem @ w_kv, 2, axis=-1)
    k = k.reshape(B, M, MEM_HEADS, MEM_HEAD_DIM)
    v = v.reshape(B, M, MEM_HEADS, MEM_HEAD_DIM)
    s = jnp.einsum('bshd,bmhd->bhsm', q, k).astype(jnp.float32) * (MEM_HEAD_DIM ** -0.5)
    p = jax.nn.softmax(s, axis=-1)
    o = jnp.einsum('bhsm,bmhd->bshd', p.astype(v.dtype), v)
    return o.reshape(q.shape[0], q.shape[1], MEM_WIDTH)


def setup_inputs(seed: int = 0) -> dict:
    key = jax.random.key(seed)
    ks = jax.random.split(key, 40)
    f32 = jnp.float32
    L = DEPTH

    def w(k, shape, fan_in, extra=1.0):
        return jax.random.normal(k, shape, f32) * (fan_in ** -0.5) * extra

    def gain(k, shape):
        return 1.0 + 0.02 * jax.random.normal(k, shape, f32)

    def bias(k, shape):
        return 0.02 * jax.random.normal(k, shape, f32)

    return {
        "x": jax.random.normal(ks[0], (BATCH, SEQ, D_MODEL), f32),
        "mem": jax.random.normal(ks[1], (BATCH, N_MEM, D_MODEL), f32),
        "positions": jnp.broadcast_to(jnp.arange(SEQ, dtype=jnp.int32), (BATCH, SEQ)),
        "ffn1_w_in": w(ks[2], (L, D_MODEL, 2 * D_FF), D_MODEL),
        "ffn1_w_out": w(ks[3], (L, D_FF, D_MODEL), D_FF, DEEPNORM_BETA),
        "ln1_g": gain(ks[4], (L, D_MODEL)),
        "ln1_b": bias(ks[5], (L, D_MODEL)),
        "w_in": w(ks[6], (L, D_MODEL, IN_WIDTH), D_MODEL),
        "gate_b": bias(ks[7], (L, N_BRANCH * D_MODEL)),
        "gm_ln_g": gain(ks[8], (L, GM_WIDTH)),
        "gm_ln_b": bias(ks[9], (L, GM_WIDTH)),
        "gm_w_s": w(ks[10], (L, GM_GROUPS, GM_CHUNK, GM_CHUNK), GM_CHUNK),
        "gm_b_s": gain(ks[11], (L, GM_GROUPS, GM_CHUNK)),
        "lambda_q1": 0.1 * jax.random.normal(ks[12], (L, DIFF_HEAD_DIM), f32),
        "lambda_k1": 0.1 * jax.random.normal(ks[13], (L, DIFF_HEAD_DIM), f32),
        "lambda_q2": 0.1 * jax.random.normal(ks[14], (L, DIFF_HEAD_DIM), f32),
        "lambda_k2": 0.1 * jax.random.normal(ks[15], (L, DIFF_HEAD_DIM), f32),
        "diff_norm_g": gain(ks[16], (L, DIFF_V_DIM)),
        "w_mem_kv": w(ks[17], (L, D_MODEL, 2 * MEM_WIDTH), D_MODEL),
        "w_branch_gm": w(ks[18], (L, GM_WIDTH, D_MODEL), GM_WIDTH),
        "w_branch_diff": w(ks[19], (L, DIFF_V_WIDTH, D_MODEL), DIFF_V_WIDTH),
        "w_branch_mem": w(ks[20], (L, MEM_WIDTH, D_MODEL), MEM_WIDTH),
        "w_o": w(ks[21], (L, D_MODEL, D_MODEL), D_MODEL, DEEPNORM_BETA),
        "ln2_g": gain(ks[22], (L, D_MODEL)),
        "ln2_b": bias(ks[23], (L, D_MODEL)),
        "ffn2_w_in": w(ks[24], (L, D_MODEL, 2 * D_FF), D_MODEL),
        "ffn2_w_out": w(ks[25], (L, D_FF, D_MODEL), D_FF, DEEPNORM_BETA),
        "ln3_g": gain(ks[26], (L, D_MODEL)),
        "ln3_b": bias(ks[27], (L, D_MODEL)),
    }


def reference(x, mem, positions, ffn1_w_in, ffn1_w_out, ln1_g, ln1_b, w_in, gate_b,
              gm_ln_g, gm_ln_b, gm_w_s, gm_b_s, lambda_q1, lambda_k1, lambda_q2, lambda_k2,
              diff_norm_g, w_mem_kv, w_branch_gm, w_branch_diff, w_branch_mem, w_o,
              ln2_g, ln2_b, ffn2_w_in, ffn2_w_out, ln3_g, ln3_b):
    B, S, _ = x.shape
    cos, sin = rope_tables(positions)
    offsets = [int(o) for o in np.cumsum(IN_SPLITS)]
    for i in range(DEPTH):
        lam_init = 0.8 - 0.6 * math.exp(-0.3 * i)
        x = layer_norm(DEEPNORM_ALPHA * x + 0.5 * swiglu(x, ffn1_w_in[i], ffn1_w_out[i]), ln1_g[i], ln1_b[i])

        h = x @ w_in[i]
        u_gm, v_gm, q_d, k_d, v_d, q_m, gate_logits = jnp.split(h, offsets, axis=-1)

        y_gm = gmlp_branch(jax.nn.gelu(u_gm), jax.nn.gelu(v_gm), gm_ln_g[i], gm_ln_b[i], gm_w_s[i], gm_b_s[i])

        q_d = apply_partial_rope(q_d.reshape(B, S, DIFF_HEADS, 2, DIFF_HEAD_DIM), cos, sin)
        k_d = apply_partial_rope(k_d.reshape(B, S, DIFF_HEADS, 2, DIFF_HEAD_DIM), cos, sin)
        v_d = v_d.reshape(B, S, DIFF_HEADS, DIFF_V_DIM)
        lam = (jnp.exp(jnp.sum(lambda_q1[i].astype(jnp.float32) * lambda_k1[i].astype(jnp.float32)))
               - jnp.exp(jnp.sum(lambda_q2[i].astype(jnp.float32) * lambda_k2[i].astype(jnp.float32)))
               + lam_init)
        y_diff = diff_attention(q_d, k_d, v_d, lam, lam_init, diff_norm_g[i])

        y_mem = memory_attention(q_m.reshape(B, S, MEM_HEADS, MEM_HEAD_DIM), mem, w_mem_kv[i])

        g = jax.nn.sigmoid(gate_logits + gate_b[i]).reshape(B, S, N_BRANCH, D_MODEL)
        merged = (g[:, :, 0] * (y_gm @ w_branch_gm[i])
                  + g[:, :, 1] * (y_diff @ w_branch_diff[i])
                  + g[:, :, 2] * (y_mem @ w_branch_mem[i]))
        x = layer_norm(DEEPNORM_ALPHA * x + merged @ w_o[i], ln2_g[i], ln2_b[i])

        x = layer_norm(DEEPNORM_ALPHA * x + 0.5 * swiglu(x, ffn2_w_in[i], ffn2_w_out[i]), ln3_g[i], ln3_b[i])
    return x
```

```python
import functools
import math

import jax
import jax.numpy as jnp
from jax import lax
from jax.experimental import pallas as pl
from jax.experimental.pallas import tpu as pltpu

F32 = jnp.float32
BF16 = jnp.bfloat16

D_MODEL = 1024
D_FF = 2816
GM_WIDTH = 512
GM_GROUPS = 4
GM_GROUP_DIM = GM_WIDTH // GM_GROUPS
GM_CHUNK = 128
DIFF_HEADS = 4
DIFF_HEAD_DIM = 64
DIFF_V_DIM = 2 * DIFF_HEAD_DIM
DIFF_Q_WIDTH = DIFF_HEADS * 2 * DIFF_HEAD_DIM
DIFF_V_WIDTH = DIFF_HEADS * DIFF_V_DIM
MEM_HEADS = 4
MEM_HEAD_DIM = 64
MEM_WIDTH = MEM_HEADS * MEM_HEAD_DIM
N_BRANCH = 3
ROPE_THETA = 500000.0
ROPE_DIM = DIFF_HEAD_DIM // 4
ROPE_HALF = ROPE_DIM // 2
DEPTH = 1
DEEPNORM_ALPHA = (2 * DEPTH) ** 0.25
LN_EPS = 1e-5
LAM_INIT = 0.8 - 0.6 * math.exp(-0.3 * 0)
OFF_U = 0
OFF_V = OFF_U + GM_WIDTH
OFF_Q = OFF_V + GM_WIDTH
OFF_K = OFF_Q + DIFF_Q_WIDTH
OFF_VD = OFF_K + DIFF_Q_WIDTH
OFF_QM = OFF_VD + DIFF_V_WIDTH
OFF_GATE = OFF_QM + MEM_WIDTH

LANES = 128
VMEM_LIMIT = 52 * 1024 * 1024

TM_FFN = 256
TM_PROJ = 256
TM_MERGE = 256
TQ = 256

NT_DIMS = (((1,), (1,)), ((), ()))


def _const_spec(shape):
    nd = len(shape)
    return pl.BlockSpec(shape, lambda *_: (0,) * nd, pipeline_mode=pl.Buffered(1))


def _params(n_axes):
    return pltpu.CompilerParams(dimension_semantics=("parallel",) * n_axes,
                                vmem_limit_bytes=VMEM_LIMIT)


def _layer_norm(z, g, b):
    mu = jnp.mean(z, axis=-1, keepdims=True)
    zc = z - mu
    var = jnp.mean(zc * zc, axis=-1, keepdims=True)
    return zc * lax.rsqrt(var + LN_EPS) * g + b


def _ffn_ln_body(x_ref, win_ref, wout_ref, g_ref, b_ref, o_ref):
    x = x_ref[...]
    xb = x.astype(BF16)
    hg = jnp.dot(xb, win_ref[:, :D_FF], preferred_element_type=F32)
    hu = jnp.dot(xb, win_ref[:, D_FF:], preferred_element_type=F32)
    h = (hg * jax.nn.sigmoid(hg) * hu).astype(BF16)
    y = jnp.dot(h, wout_ref[...], preferred_element_type=F32)
    o_ref[...] = _layer_norm(DEEPNORM_ALPHA * x + 0.5 * y, g_ref[...], b_ref[...])


def _ffn_ln(x2d, w_in, w_out, g, b):
    t = x2d.shape[0]
    return pl.pallas_call(
        _ffn_ln_body,
        grid=(t // TM_FFN,),
        in_specs=[
            pl.BlockSpec((TM_FFN, D_MODEL), lambda i: (i, 0)),
            _const_spec((D_MODEL, 2 * D_FF)),
            _const_spec((D_FF, D_MODEL)),
            _const_spec((1, D_MODEL)),
            _const_spec((1, D_MODEL)),
        ],
        out_specs=pl.BlockSpec((TM_FFN, D_MODEL), lambda i: (i, 0)),
        out_shape=jax.ShapeDtypeStruct((t, D_MODEL), F32),
        compiler_params=_params(1),
        name="ffn_ln",
    )(x2d, w_in, w_out, g, b)


def _mem_kv_body(m_ref, w_ref, k_ref, v_ref):
    kv = jnp.dot(m_ref[...].astype(BF16), w_ref[...], preferred_element_type=F32)
    k_ref[...] = kv[:, :MEM_WIDTH].astype(BF16)
    v_ref[...] = kv[:, MEM_WIDTH:].astype(BF16)


def _mem_kv(mem2d, w_kv, n_mem):
    rows = mem2d.shape[0]
    return pl.pallas_call(
        _mem_kv_body,
        grid=(rows // n_mem,),
        in_specs=[
            pl.BlockSpec((n_mem, D_MODEL), lambda i: (i, 0)),
            _const_spec((D_MODEL, 2 * MEM_WIDTH)),
        ],
        out_specs=[pl.BlockSpec((n_mem, MEM_WIDTH), lambda i: (i, 0))] * 2,
        out_shape=[jax.ShapeDtypeStruct((rows, MEM_WIDTH), BF16)] * 2,
        compiler_params=_params(1),
        name="mem_kv",
    )(mem2d, w_kv)


def _proj_body(x_ref, pos_ref, invf_ref, w_ref, gmg_ref, gmb_ref, ws_ref, bst_ref, km_ref, vm_ref,
               q_ref, k_ref, v_ref, ygm_ref, ymem_ref):
    tm = x_ref.shape[0]
    xb = x_ref[...].astype(BF16)
    h = jnp.dot(xb, w_ref[...], preferred_element_type=F32)

    u = jax.nn.gelu(h[:, OFF_U:OFF_U + GM_WIDTH])
    v = jax.nn.gelu(h[:, OFF_V:OFF_V + GM_WIDTH])
    vn = _layer_norm(v, gmg_ref[...], gmb_ref[...]).astype(BF16)
    row = lax.broadcasted_iota(jnp.int32, (GM_CHUNK, GM_CHUNK), 0)
    col = lax.broadcasted_iota(jnp.int32, (GM_CHUNK, GM_CHUNK), 1)
    causal = col <= row
    for g in range(GM_GROUPS):
        wc = jnp.where(causal, ws_ref[g], 0.0).astype(BF16)
        bcol = bst_ref[:, g:g + 1]
        gs = slice(g * GM_GROUP_DIM, (g + 1) * GM_GROUP_DIM)
        for c in range(tm // GM_CHUNK):
            rs = slice(c * GM_CHUNK, (c + 1) * GM_CHUNK)
            mixed = jnp.dot(wc, vn[rs, gs], preferred_element_type=F32) + bcol
            ygm_ref[rs, gs] = (u[rs, gs] * mixed).astype(BF16)

    ang = pos_ref[...].astype(F32) * invf_ref[...]
    cos = jnp.cos(ang)
    sin = jnp.sin(ang)
    sub = lax.broadcasted_iota(jnp.int32, (tm, LANES), 1) % DIFF_HEAD_DIM
    sin_lo = jnp.where(sub < ROPE_HALF, -sin, 0.0)
    sin_hi = jnp.where(sub >= ROPE_HALF, sin, 0.0)

    def rope(t):
        return (t * cos + pltpu.roll(t, LANES - ROPE_HALF, axis=1) * sin_lo
                + pltpu.roll(t, ROPE_HALF, axis=1) * sin_hi)

    scale = DIFF_HEAD_DIM ** -0.5
    for j in range(DIFF_Q_WIDTH // LANES):
        cs = slice(j * LANES, (j + 1) * LANES)
        q_ref[:, cs] = (rope(h[:, OFF_Q + j * LANES:OFF_Q + (j + 1) * LANES]) * scale).astype(BF16)
        k_ref[:, cs] = rope(h[:, OFF_K + j * LANES:OFF_K + (j + 1) * LANES]).astype(BF16)
    v_ref[...] = h[:, OFF_VD:OFF_VD + DIFF_V_WIDTH].astype(BF16)

    mscale = MEM_HEAD_DIM ** -0.5
    outs = []
    for hh in range(MEM_HEADS):
        hs = slice(hh * MEM_HEAD_DIM, (hh + 1) * MEM_HEAD_DIM)
        qh = (h[:, OFF_QM + hh * MEM_HEAD_DIM:OFF_QM + (hh + 1) * MEM_HEAD_DIM] * mscale).astype(BF16)
        s = lax.dot_general(qh, km_ref[:, hs], NT_DIMS, preferred_element_type=F32)
        e = jnp.exp(s - jnp.max(s, axis=-1, keepdims=True))
        p = e * (1.0 / jnp.sum(e, axis=-1, keepdims=True))
        outs.append(jnp.dot(p.astype(BF16), vm_ref[:, hs], preferred_element_type=F32))
    ymem_ref[...] = jnp.concatenate(outs, axis=-1).astype(BF16)


def _proj(x2d, pos2d, invf, w, gm_g, gm_b, w_s, b_st, km, vm, seq, n_mem):
    t = x2d.shape[0]
    tiles_per_batch = seq // TM_PROJ
    row_spec = lambda width: pl.BlockSpec((TM_PROJ, width), lambda i: (i, 0))
    mem_spec = pl.BlockSpec((n_mem, MEM_WIDTH), lambda i: (i // tiles_per_batch, 0))
    return pl.pallas_call(
        _proj_body,
        grid=(t // TM_PROJ,),
        in_specs=[
            row_spec(D_MODEL),
            row_spec(1),
            _const_spec((1, LANES)),
            _const_spec((D_MODEL, OFF_GATE)),
            _const_spec((1, GM_WIDTH)),
            _const_spec((1, GM_WIDTH)),
            _const_spec((GM_GROUPS, GM_CHUNK, GM_CHUNK)),
            _const_spec((GM_CHUNK, GM_GROUPS)),
            mem_spec,
            mem_spec,
        ],
        out_specs=[row_spec(DIFF_Q_WIDTH), row_spec(DIFF_Q_WIDTH), row_spec(DIFF_V_WIDTH),
                   row_spec(GM_WIDTH), row_spec(MEM_WIDTH)],
        out_shape=[jax.ShapeDtypeStruct((t, DIFF_Q_WIDTH), BF16),
                   jax.ShapeDtypeStruct((t, DIFF_Q_WIDTH), BF16),
                   jax.ShapeDtypeStruct((t, DIFF_V_WIDTH), BF16),
                   jax.ShapeDtypeStruct((t, GM_WIDTH), BF16),
                   jax.ShapeDtypeStruct((t, MEM_WIDTH), BF16)],
        compiler_params=_params(1),
        name="proj",
    )(x2d, pos2d, invf, w, gm_g, gm_b, w_s, b_st, km, vm)


def _diff_attn_body(lamv_ref, ng_ref, q_ref, k_ref, v_ref, o_ref, m_ref, l_ref, acc_ref):
    qi = pl.program_id(2)
    q = q_ref[0]
    qs = (q[:, :DIFF_HEAD_DIM], q[:, DIFF_HEAD_DIM:])
    m_ref[...] = jnp.full(m_ref.shape, -jnp.inf, F32)
    l_ref[...] = jnp.zeros(l_ref.shape, F32)
    acc_ref[...] = jnp.zeros(acc_ref.shape, F32)

    def block(ki, masked):
        start = pl.multiple_of(ki * TQ, TQ)
        kb = k_ref[0, pl.ds(start, TQ), :]
        vb = v_ref[0, pl.ds(start, TQ), :]
        for r in range(2):
            kr = kb[:, r * DIFF_HEAD_DIM:(r + 1) * DIFF_HEAD_DIM]
            s = lax.dot_general(qs[r], kr, NT_DIMS, preferred_element_type=F32)
            if masked:
                row = lax.broadcasted_iota(jnp.int32, (TQ, TQ), 0)
                col = lax.broadcasted_iota(jnp.int32, (TQ, TQ), 1)
                s = jnp.where(col <= row, s, -jnp.inf)
            m_old = m_ref[r]
            m_new = jnp.maximum(m_old, jnp.max(s, axis=-1, keepdims=True))
            a = jnp.exp(m_old - m_new)
            e = jnp.exp(s - m_new)
            l_ref[r] = a * l_ref[r] + jnp.sum(e, axis=-1, keepdims=True)
            acc_ref[r] = a * acc_ref[r] + jnp.dot(e.astype(BF16), vb, preferred_element_type=F32)
            m_ref[r] = m_new

    def body(ki, carry):
        block(ki, False)
        return carry

    lax.fori_loop(0, qi, body, 0)
    block(qi, True)

    lv = lamv_ref[...]
    lam = (jnp.exp(jnp.sum(lv[0:1] * lv[1:2], axis=-1, keepdims=True))
           - jnp.exp(jnp.sum(lv[2:3] * lv[3:4], axis=-1, keepdims=True)) + LAM_INIT)
    o = acc_ref[0] / l_ref[0] - lam * (acc_ref[1] / l_ref[1])
    o = o * lax.rsqrt(jnp.mean(o * o, axis=-1, keepdims=True) + LN_EPS) * ng_ref[...]
    o_ref[0] = (o * (1.0 - LAM_INIT)).astype(BF16)


def _diff_attn(lamv, norm_g, q, k, v):
    b, s, _ = q.shape
    blk = lambda rows, imap: pl.BlockSpec((1, rows, DIFF_V_DIM), imap)
    return pl.pallas_call(
        _diff_attn_body,
        grid=(b, DIFF_HEADS, s // TQ),
        in_specs=[
            _const_spec((4, DIFF_HEAD_DIM)),
            _const_spec((1, DIFF_V_DIM)),
            blk(TQ, lambda bi, hi, qi: (bi, qi, hi)),
            blk(s, lambda bi, hi, qi: (bi, 0, hi)),
            blk(s, lambda bi, hi, qi: (bi, 0, hi)),
        ],
        out_specs=blk(TQ, lambda bi, hi, qi: (bi, qi, hi)),
        out_shape=jax.ShapeDtypeStruct((b, s, DIFF_V_WIDTH), BF16),
        scratch_shapes=[pltpu.VMEM((2, TQ, 1), F32), pltpu.VMEM((2, TQ, 1), F32),
                        pltpu.VMEM((2, TQ, DIFF_V_DIM), F32)],
        compiler_params=_params(3),
        name="diff_attn",
    )(lamv, norm_g, q, k, v)


def _merge_body(x_ref, ygm_ref, ydf_ref, ymm_ref, wgate_ref, gb_ref, wgm_ref, wdf_ref, wmm_ref, wo_ref,
                g_ref, b_ref, o_ref):
    x = x_ref[...]
    gates = jax.nn.sigmoid(jnp.dot(x.astype(BF16), wgate_ref[...], preferred_element_type=F32) + gb_ref[...])
    merged = (gates[:, :D_MODEL] * jnp.dot(ygm_ref[...], wgm_ref[...], preferred_element_type=F32)
              + gates[:, D_MODEL:2 * D_MODEL] * jnp.dot(ydf_ref[...], wdf_ref[...], preferred_element_type=F32)
              + gates[:, 2 * D_MODEL:] * jnp.dot(ymm_ref[...], wmm_ref[...], preferred_element_type=F32))
    y = jnp.dot(merged.astype(BF16), wo_ref[...], preferred_element_type=F32)
    o_ref[...] = _layer_norm(DEEPNORM_ALPHA * x + y, g_ref[...], b_ref[...])


def _merge(x2d, ygm, ydf, ymm, w_gate, gate_b, w_gm, w_df, w_mm, w_o, g, b):
    t = x2d.shape[0]
    row_spec = lambda width: pl.BlockSpec((TM_MERGE, width), lambda i: (i, 0))
    return pl.pallas_call(
        _merge_body,
        grid=(t // TM_MERGE,),
        in_specs=[
            row_spec(D_MODEL), row_spec(GM_WIDTH), row_spec(DIFF_V_WIDTH), row_spec(MEM_WIDTH),
            _const_spec((D_MODEL, N_BRANCH * D_MODEL)),
            _const_spec((1, N_BRANCH * D_MODEL)),
            _const_spec((GM_WIDTH, D_MODEL)),
            _const_spec((DIFF_V_WIDTH, D_MODEL)),
            _const_spec((MEM_WIDTH, D_MODEL)),
            _const_spec((D_MODEL, D_MODEL)),
            _const_spec((1, D_MODEL)),
            _const_spec((1, D_MODEL)),
        ],
        out_specs=row_spec(D_MODEL),
        out_shape=jax.ShapeDtypeStruct((t, D_MODEL), F32),
        compiler_params=_params(1),
        name="merge",
    )(x2d, ygm, ydf, ymm, w_gate, gate_b, w_gm, w_df, w_mm, w_o, g, b)


def _rope_lane_freqs():
    inv_freq = ROPE_THETA ** (-jnp.arange(0, ROPE_DIM, 2, dtype=F32) / ROPE_DIM)
    sub = jnp.arange(LANES) % DIFF_HEAD_DIM
    return jnp.where(sub < ROPE_DIM, inv_freq[sub % ROPE_HALF], 0.0).astype(F32).reshape(1, LANES)


def kernel(x, mem, positions, ffn1_w_in, ffn1_w_out, ln1_g, ln1_b, w_in, gate_b, gm_ln_g, gm_ln_b, gm_w_s, gm_b_s,
           lambda_q1, lambda_k1, lambda_q2, lambda_k2, diff_norm_g, w_mem_kv, w_branch_gm, w_branch_diff,
           w_branch_mem, w_o, ln2_g, ln2_b, ffn2_w_in, ffn2_w_out, ln3_g, ln3_b):
    bsz, seq, _ = x.shape
    n_mem = mem.shape[1]
    t = bsz * seq
    i = 0
    x2d = x.reshape(t, D_MODEL)

    x1 = _ffn_ln(x2d, ffn1_w_in[i].astype(BF16), ffn1_w_out[i].astype(BF16), ln1_g[i][None], ln1_b[i][None])

    km, vm = _mem_kv(mem.reshape(bsz * n_mem, D_MODEL), w_mem_kv[i].astype(BF16), n_mem)
    q, k, v, ygm, ymm = _proj(
        x1, positions.reshape(t, 1), _rope_lane_freqs(), w_in[i][:, :OFF_GATE].astype(BF16),
        gm_ln_g[i][None], gm_ln_b[i][None], gm_w_s[i], gm_b_s[i].T, km, vm, seq, n_mem)

    lamv = jnp.stack([lambda_q1[i], lambda_k1[i], lambda_q2[i], lambda_k2[i]]).astype(F32)
    ydf = _diff_attn(lamv, diff_norm_g[i][None],
                     q.reshape(bsz, seq, DIFF_Q_WIDTH), k.reshape(bsz, seq, DIFF_Q_WIDTH),
                     v.reshape(bsz, seq, DIFF_V_WIDTH)).reshape(t, DIFF_V_WIDTH)

    x2 = _merge(x1, ygm, ydf, ymm, w_in[i][:, OFF_GATE:].astype(BF16), gate_b[i][None],
                w_branch_gm[i].astype(BF16), w_branch_diff[i].astype(BF16), w_branch_mem[i].astype(BF16),
                w_o[i].astype(BF16), ln2_g[i][None], ln2_b[i][None])

    x3 = _ffn_ln(x2, ffn2_w_in[i].astype(BF16), ffn2_w_out[i].astype(BF16), ln3_g[i][None], ln3_b[i][None])
    return x3.reshape(bsz, seq, D_MODEL)
```

```python
import functools
import math

import jax
import jax.numpy as jnp
from jax import lax
from jax.experimental import pallas as pl
from jax.experimental.pallas import tpu as pltpu

F32 = jnp.float32
BF16 = jnp.bfloat16

D_MODEL = 1024
D_FF = 2816
GM_WIDTH = 512
GM_GROUPS = 4
GM_GROUP_DIM = GM_WIDTH // GM_GROUPS
GM_CHUNK = 128
DIFF_HEADS = 4
DIFF_HEAD_DIM = 64
DIFF_V_DIM = 2 * DIFF_HEAD_DIM
DIFF_Q_WIDTH = DIFF_HEADS * 2 * DIFF_HEAD_DIM
DIFF_V_WIDTH = DIFF_HEADS * DIFF_V_DIM
MEM_HEADS = 4
MEM_HEAD_DIM = 64
MEM_WIDTH = MEM_HEADS * MEM_HEAD_DIM
N_BRANCH = 3
ROPE_THETA = 500000.0
ROPE_DIM = DIFF_HEAD_DIM // 4
ROPE_HALF = ROPE_DIM // 2
DEPTH = 1
DEEPNORM_ALPHA = (2 * DEPTH) ** 0.25
LN_EPS = 1e-5
LAM_INIT = 0.8 - 0.6 * math.exp(-0.3 * 0)
OFF_U = 0
OFF_V = OFF_U + GM_WIDTH
OFF_Q = OFF_V + GM_WIDTH
OFF_K = OFF_Q + DIFF_Q_WIDTH
OFF_VD = OFF_K + DIFF_Q_WIDTH
OFF_QM = OFF_VD + DIFF_V_WIDTH
OFF_GATE = OFF_QM + MEM_WIDTH

LANES = 128
VMEM_LIMIT = 52 * 1024 * 1024

TM_FFN = 256
TM_PROJ = 256
TM_MERGE = 256
TQ = 512

NT_DIMS = (((1,), (1,)), ((), ()))


def _const_spec(shape):
    nd = len(shape)
    return pl.BlockSpec(shape, lambda *_: (0,) * nd, pipeline_mode=pl.Buffered(1))


def _params(n_axes):
    return pltpu.CompilerParams(dimension_semantics=("parallel",) * n_axes,
                                vmem_limit_bytes=VMEM_LIMIT)


def _layer_norm(z, g, b):
    mu = jnp.mean(z, axis=-1, keepdims=True)
    zc = z - mu
    var = jnp.mean(zc * zc, axis=-1, keepdims=True)
    return zc * lax.rsqrt(var + LN_EPS) * g + b


def _ffn_ln_body(x_ref, win_ref, wout_ref, g_ref, b_ref, o_ref):
    x = x_ref[...]
    xb = x.astype(BF16)
    hg = jnp.dot(xb, win_ref[:, :D_FF], preferred_element_type=F32)
    hu = jnp.dot(xb, win_ref[:, D_FF:], preferred_element_type=F32)
    h = (hg * jax.nn.sigmoid(hg) * hu).astype(BF16)
    y = jnp.dot(h, wout_ref[...], preferred_element_type=F32)
    o_ref[...] = _layer_norm(DEEPNORM_ALPHA * x + 0.5 * y, g_ref[...], b_ref[...])


def _ffn_ln(x2d, w_in, w_out, g, b):
    t = x2d.shape[0]
    return pl.pallas_call(
        _ffn_ln_body,
        grid=(t // TM_FFN,),
        in_specs=[
            pl.BlockSpec((TM_FFN, D_MODEL), lambda i: (i, 0)),
            _const_spec((D_MODEL, 2 * D_FF)),
            _const_spec((D_FF, D_MODEL)),
            _const_spec((1, D_MODEL)),
            _const_spec((1, D_MODEL)),
        ],
        out_specs=pl.BlockSpec((TM_FFN, D_MODEL), lambda i: (i, 0)),
        out_shape=jax.ShapeDtypeStruct((t, D_MODEL), F32),
        compiler_params=_params(1),
        name="ffn_ln",
    )(x2d, w_in, w_out, g, b)


def _mem_kv_body(m_ref, w_ref, k_ref, v_ref):
    kv = jnp.dot(m_ref[...].astype(BF16), w_ref[...], preferred_element_type=F32)
    k_ref[...] = kv[:, :MEM_WIDTH].astype(BF16)
    v_ref[...] = kv[:, MEM_WIDTH:].astype(BF16)


def _mem_kv(mem2d, w_kv, n_mem):
    rows = mem2d.shape[0]
    return pl.pallas_call(
        _mem_kv_body,
        grid=(rows // n_mem,),
        in_specs=[
            pl.BlockSpec((n_mem, D_MODEL), lambda i: (i, 0)),
            _const_spec((D_MODEL, 2 * MEM_WIDTH)),
        ],
        out_specs=[pl.BlockSpec((n_mem, MEM_WIDTH), lambda i: (i, 0))] * 2,
        out_shape=[jax.ShapeDtypeStruct((rows, MEM_WIDTH), BF16)] * 2,
        compiler_params=_params(1),
        name="mem_kv",
    )(mem2d, w_kv)


def _proj_body(x_ref, pos_ref, invf_ref, w_ref, gmg_ref, gmb_ref, ws_ref, bst_ref, km_ref, vm_ref,
               q_ref, k_ref, vt_ref, ygm_ref, ymem_ref):
    tm = x_ref.shape[0]
    xb = x_ref[...].astype(BF16)
    h = jnp.dot(xb, w_ref[...], preferred_element_type=F32)

    u = jax.nn.gelu(h[:, OFF_U:OFF_U + GM_WIDTH])
    v = jax.nn.gelu(h[:, OFF_V:OFF_V + GM_WIDTH])
    vn = _layer_norm(v, gmg_ref[...], gmb_ref[...]).astype(BF16)
    row = lax.broadcasted_iota(jnp.int32, (GM_CHUNK, GM_CHUNK), 0)
    col = lax.broadcasted_iota(jnp.int32, (GM_CHUNK, GM_CHUNK), 1)
    causal = col <= row
    for g in range(GM_GROUPS):
        wc = jnp.where(causal, ws_ref[g], 0.0).astype(BF16)
        bcol = bst_ref[:, g:g + 1]
        gs = slice(g * GM_GROUP_DIM, (g + 1) * GM_GROUP_DIM)
        for c in range(tm // GM_CHUNK):
            rs = slice(c * GM_CHUNK, (c + 1) * GM_CHUNK)
            mixed = jnp.dot(wc, vn[rs, gs], preferred_element_type=F32) + bcol
            ygm_ref[rs, gs] = (u[rs, gs] * mixed).astype(BF16)

    ang = pos_ref[...].astype(F32) * invf_ref[...]
    cos = jnp.cos(ang)
    sin = jnp.sin(ang)
    sub = lax.broadcasted_iota(jnp.int32, (tm, LANES), 1) % DIFF_HEAD_DIM
    sin_lo = jnp.where(sub < ROPE_HALF, -sin, 0.0)
    sin_hi = jnp.where(sub >= ROPE_HALF, sin, 0.0)

    def rope(t):
        return (t * cos + pltpu.roll(t, LANES - ROPE_HALF, axis=1) * sin_lo
                + pltpu.roll(t, ROPE_HALF, axis=1) * sin_hi)

    scale = DIFF_HEAD_DIM ** -0.5
    for j in range(DIFF_Q_WIDTH // LANES):
        cs = slice(j * LANES, (j + 1) * LANES)
        q_ref[:, cs] = (rope(h[:, OFF_Q + j * LANES:OFF_Q + (j + 1) * LANES]) * scale).astype(BF16)
        k_ref[:, cs] = rope(h[:, OFF_K + j * LANES:OFF_K + (j + 1) * LANES]).astype(BF16)
    vt_ref[0] = h[:, OFF_VD:OFF_VD + DIFF_V_WIDTH].T.astype(BF16)

    mscale = MEM_HEAD_DIM ** -0.5
    outs = []
    for hh in range(MEM_HEADS):
        hs = slice(hh * MEM_HEAD_DIM, (hh + 1) * MEM_HEAD_DIM)
        qh = (h[:, OFF_QM + hh * MEM_HEAD_DIM:OFF_QM + (hh + 1) * MEM_HEAD_DIM] * mscale).astype(BF16)
        s = lax.dot_general(qh, km_ref[:, hs], NT_DIMS, preferred_element_type=F32)
        e = jnp.exp(s - jnp.max(s, axis=-1, keepdims=True))
        p = e * (1.0 / jnp.sum(e, axis=-1, keepdims=True))
        outs.append(jnp.dot(p.astype(BF16), vm_ref[:, hs], preferred_element_type=F32))
    ymem_ref[...] = jnp.concatenate(outs, axis=-1).astype(BF16)


def _proj(x2d, pos2d, invf, w, gm_g, gm_b, w_s, b_st, km, vm, seq, n_mem):
    t = x2d.shape[0]
    tiles_per_batch = seq // TM_PROJ
    row_spec = lambda width: pl.BlockSpec((TM_PROJ, width), lambda i: (i, 0))
    mem_spec = pl.BlockSpec((n_mem, MEM_WIDTH), lambda i: (i // tiles_per_batch, 0))
    return pl.pallas_call(
        _proj_body,
        grid=(t // TM_PROJ,),
        in_specs=[
            row_spec(D_MODEL),
            row_spec(1),
            _const_spec((1, LANES)),
            _const_spec((D_MODEL, OFF_GATE)),
            _const_spec((1, GM_WIDTH)),
            _const_spec((1, GM_WIDTH)),
            _const_spec((GM_GROUPS, GM_CHUNK, GM_CHUNK)),
            _const_spec((GM_CHUNK, GM_GROUPS)),
            mem_spec,
            mem_spec,
        ],
        out_specs=[row_spec(DIFF_Q_WIDTH), row_spec(DIFF_Q_WIDTH),
                   pl.BlockSpec((1, DIFF_V_WIDTH, TM_PROJ),
                                lambda i: (i // tiles_per_batch, 0, i % tiles_per_batch)),
                   row_spec(GM_WIDTH), row_spec(MEM_WIDTH)],
        out_shape=[jax.ShapeDtypeStruct((t, DIFF_Q_WIDTH), BF16),
                   jax.ShapeDtypeStruct((t, DIFF_Q_WIDTH), BF16),
                   jax.ShapeDtypeStruct((t // seq, DIFF_V_WIDTH, seq), BF16),
                   jax.ShapeDtypeStruct((t, GM_WIDTH), BF16),
                   jax.ShapeDtypeStruct((t, MEM_WIDTH), BF16)],
        compiler_params=_params(1),
        name="proj",
    )(x2d, pos2d, invf, w, gm_g, gm_b, w_s, b_st, km, vm)


def _diff_attn_body(lamv_ref, ngc_ref, q_ref, k_ref, vt_ref, o_ref, st_ref, e_ref, acc_ref):
    qi = pl.program_id(2)
    q = q_ref[0]
    lane = lax.broadcasted_iota(jnp.int32, q.shape, 1)
    zero = jnp.zeros_like(q)
    q_stack = jnp.concatenate([jnp.where(lane < DIFF_HEAD_DIM, q, zero),
                               jnp.where(lane >= DIFF_HEAD_DIM, q, zero)], axis=0)
    acc_ref[...] = jnp.zeros(acc_ref.shape, F32)
    e_ref[...] = jnp.zeros(e_ref.shape, BF16)

    def scores(ki):
        kb = k_ref[0, pl.ds(pl.multiple_of(ki * TQ, TQ), TQ), :]
        st = lax.dot_general(kb, q_stack, NT_DIMS, preferred_element_type=F32)
        st_ref[...] = st
        return jnp.max(st, axis=0, keepdims=True)

    def values(ki):
        vtb = vt_ref[0, :, pl.ds(pl.multiple_of(ki * TQ, TQ), TQ)]
        return jnp.dot(vtb, e_ref[...], preferred_element_type=F32)

    def softmax(st, m, l, bm):
        m_new = jnp.maximum(m, bm)
        a = jnp.exp(m - m_new)
        e = jnp.exp(st - m_new)
        e_ref[...] = e.astype(BF16)
        return a, m_new, a * l + jnp.sum(e, axis=0, keepdims=True)

    def step(t, carry):
        a_prev, m, l, bm = carry
        pv = values(jnp.maximum(t - 1, 0))
        a, m, l = softmax(st_ref[...], m, l, bm)
        acc_ref[...] = a_prev * acc_ref[...] + pv
        return a, m, l, scores(t + 1)

    init = (jnp.ones((1, 2 * TQ), F32), jnp.full((1, 2 * TQ), -jnp.inf, F32), jnp.zeros((1, 2 * TQ), F32), scores(0))
    a_prev, m, l, _ = lax.fori_loop(0, qi, step, init)
    pv = values(jnp.maximum(qi - 1, 0))
    kpos = lax.broadcasted_iota(jnp.int32, st_ref.shape, 0)
    qpos = lax.broadcasted_iota(jnp.int32, st_ref.shape, 1) % TQ
    st = jnp.where(kpos <= qpos, st_ref[...], -jnp.inf)
    a, m, l = softmax(st, m, l, jnp.max(st, axis=0, keepdims=True))
    acc = a * (a_prev * acc_ref[...] + pv) + values(qi)

    lv = lamv_ref[...]
    lam = (jnp.exp(jnp.sum(lv[0:1] * lv[1:2], axis=-1, keepdims=True))
           - jnp.exp(jnp.sum(lv[2:3] * lv[3:4], axis=-1, keepdims=True)) + LAM_INIT)
    on = acc * (1.0 / l)
    ot = on[:, :TQ] - lam * on[:, TQ:]
    ot = ot * lax.rsqrt(jnp.mean(ot * ot, axis=0, keepdims=True) + LN_EPS) * ngc_ref[...]
    o_ref[0] = (ot * (1.0 - LAM_INIT)).T.astype(BF16)


def _diff_attn(lamv, norm_g_col, q, k, vt):
    b, s, _ = q.shape
    blk = lambda rows, imap: pl.BlockSpec((1, rows, DIFF_V_DIM), imap)
    return pl.pallas_call(
        _diff_attn_body,
        grid=(b, DIFF_HEADS, s // TQ),
        in_specs=[
            _const_spec((4, DIFF_HEAD_DIM)),
            _const_spec((DIFF_V_DIM, 1)),
            blk(TQ, lambda bi, hi, qi: (bi, qi, hi)),
            blk(s, lambda bi, hi, qi: (bi, 0, hi)),
            pl.BlockSpec((1, DIFF_V_DIM, s), lambda bi, hi, qi: (bi, hi, 0)),
        ],
        out_specs=blk(TQ, lambda bi, hi, qi: (bi, qi, hi)),
        out_shape=jax.ShapeDtypeStruct((b, s, DIFF_V_WIDTH), BF16),
        scratch_shapes=[pltpu.VMEM((TQ, 2 * TQ), F32), pltpu.VMEM((TQ, 2 * TQ), BF16),
                        pltpu.VMEM((DIFF_V_DIM, 2 * TQ), F32)],
        compiler_params=_params(3),
        name="diff_attn",
    )(lamv, norm_g_col, q, k, vt)


def _merge_body(x_ref, ygm_ref, ydf_ref, ymm_ref, wgate_ref, gb_ref, wgm_ref, wdf_ref, wmm_ref, wo_ref,
                g_ref, b_ref, o_ref):
    x = x_ref[...]
    gates = jax.nn.sigmoid(jnp.dot(x.astype(BF16), wgate_ref[...], preferred_element_type=F32) + gb_ref[...])
    merged = (gates[:, :D_MODEL] * jnp.dot(ygm_ref[...], wgm_ref[...], preferred_element_type=F32)
              + gates[:, D_MODEL:2 * D_MODEL] * jnp.dot(ydf_ref[...], wdf_ref[...], preferred_element_type=F32)
              + gates[:, 2 * D_MODEL:] * jnp.dot(ymm_ref[...], wmm_ref[...], preferred_element_type=F32))
    y = jnp.dot(merged.astype(BF16), wo_ref[...], preferred_element_type=F32)
    o_ref[...] = _layer_norm(DEEPNORM_ALPHA * x + y, g_ref[...], b_ref[...])


def _merge(x2d, ygm, ydf, ymm, w_gate, gate_b, w_gm, w_df, w_mm, w_o, g, b):
    t = x2d.shape[0]
    row_spec = lambda width: pl.BlockSpec((TM_MERGE, width), lambda i: (i, 0))
    return pl.pallas_call(
        _merge_body,
        grid=(t // TM_MERGE,),
        in_specs=[
            row_spec(D_MODEL), row_spec(GM_WIDTH), row_spec(DIFF_V_WIDTH), row_spec(MEM_WIDTH),
            _const_spec((D_MODEL, N_BRANCH * D_MODEL)),
            _const_spec((1, N_BRANCH * D_MODEL)),
            _const_spec((GM_WIDTH, D_MODEL)),
            _const_spec((DIFF_V_WIDTH, D_MODEL)),
            _const_spec((MEM_WIDTH, D_MODEL)),
            _const_spec((D_MODEL, D_MODEL)),
            _const_spec((1, D_MODEL)),
            _const_spec((1, D_MODEL)),
        ],
        out_specs=row_spec(D_MODEL),
        out_shape=jax.ShapeDtypeStruct((t, D_MODEL), F32),
        compiler_params=_params(1),
        name="merge",
    )(x2d, ygm, ydf, ymm, w_gate, gate_b, w_gm, w_df, w_mm, w_o, g, b)


def _rope_lane_freqs():
    inv_freq = ROPE_THETA ** (-jnp.arange(0, ROPE_DIM, 2, dtype=F32) / ROPE_DIM)
    sub = jnp.arange(LANES) % DIFF_HEAD_DIM
    return jnp.where(sub < ROPE_DIM, inv_freq[sub % ROPE_HALF], 0.0).astype(F32).reshape(1, LANES)


def kernel(x, mem, positions, ffn1_w_in, ffn1_w_out, ln1_g, ln1_b, w_in, gate_b, gm_ln_g, gm_ln_b, gm_w_s, gm_b_s,
           lambda_q1, lambda_k1, lambda_q2, lambda_k2, diff_norm_g, w_mem_kv, w_branch_gm, w_branch_diff,
           w_branch_mem, w_o, ln2_g, ln2_b, ffn2_w_in, ffn2_w_out, ln3_g, ln3_b):
    bsz, seq, _ = x.shape
    n_mem = mem.shape[1]
    t = bsz * seq
    i = 0
    x2d = x.reshape(t, D_MODEL)

    x1 = _ffn_ln(x2d, ffn1_w_in[i].astype(BF16), ffn1_w_out[i].astype(BF16), ln1_g[i][None], ln1_b[i][None])

    km, vm = _mem_kv(mem.reshape(bsz * n_mem, D_MODEL), w_mem_kv[i].astype(BF16), n_mem)
    q, k, vt, ygm, ymm = _proj(
        x1, positions.reshape(t, 1), _rope_lane_freqs(), w_in[i][:, :OFF_GATE].astype(BF16),
        gm_ln_g[i][None], gm_ln_b[i][None], gm_w_s[i], gm_b_s[i].T, km, vm, seq, n_mem)

    lamv = jnp.stack([lambda_q1[i], lambda_k1[i], lambda_q2[i], lambda_k2[i]]).astype(F32)
    ydf = _diff_attn(lamv, diff_norm_g[i][:, None],
                     q.reshape(bsz, seq, DIFF_Q_WIDTH), k.reshape(bsz, seq, DIFF_Q_WIDTH),
                     vt).reshape(t, DIFF_V_WIDTH)

    x2 = _merge(x1, ygm, ydf, ymm, w_in[i][:, OFF_GATE:].astype(BF16), gate_b[i][None],
                w_branch_gm[i].astype(BF16), w_branch_diff[i].astype(BF16), w_branch_mem[i].astype(BF16),
                w_o[i].astype(BF16), ln2_g[i][None], ln2_b[i][None])

    x3 = _ffn_ln(x2, ffn2_w_in[i].astype(BF16), ffn2_w_out[i].astype(BF16), ln3_g[i][None], ln3_b[i][None])
    return x3.reshape(bsz, seq, D_MODEL)
```

```python
import functools
import math

import jax
import jax.numpy as jnp
from jax import lax
from jax.experimental import pallas as pl
from jax.experimental.pallas import tpu as pltpu

F32 = jnp.float32
BF16 = jnp.bfloat16

D_MODEL = 1024
D_FF = 2816
GM_WIDTH = 512
GM_GROUPS = 4
GM_GROUP_DIM = GM_WIDTH // GM_GROUPS
GM_CHUNK = 128
DIFF_HEADS = 4
DIFF_HEAD_DIM = 64
DIFF_V_DIM = 2 * DIFF_HEAD_DIM
DIFF_Q_WIDTH = DIFF_HEADS * 2 * DIFF_HEAD_DIM
DIFF_V_WIDTH = DIFF_HEADS * DIFF_V_DIM
MEM_HEADS = 4
MEM_HEAD_DIM = 64
MEM_WIDTH = MEM_HEADS * MEM_HEAD_DIM
N_BRANCH = 3
ROPE_THETA = 500000.0
ROPE_DIM = DIFF_HEAD_DIM // 4
ROPE_HALF = ROPE_DIM // 2
DEPTH = 1
DEEPNORM_ALPHA = (2 * DEPTH) ** 0.25
LN_EPS = 1e-5
LAM_INIT = 0.8 - 0.6 * math.exp(-0.3 * 0)
OFF_U = 0
OFF_V = OFF_U + GM_WIDTH
OFF_Q = OFF_V + GM_WIDTH
OFF_K = OFF_Q + DIFF_Q_WIDTH
OFF_VD = OFF_K + DIFF_Q_WIDTH
OFF_QM = OFF_VD + DIFF_V_WIDTH
OFF_GATE = OFF_QM + MEM_WIDTH

LANES = 128
BF16_SUBLANES = 16
VT_ROWS = DIFF_V_DIM + BF16_SUBLANES
VMEM_LIMIT = 52 * 1024 * 1024

TM_FFN = 256
TM_PROJ = 256
TM_MERGE = 256
TQ = 512

NT_DIMS = (((1,), (1,)), ((), ()))


def _const_spec(shape):
    nd = len(shape)
    return pl.BlockSpec(shape, lambda *_: (0,) * nd, pipeline_mode=pl.Buffered(1))


def _params(n_axes):
    return pltpu.CompilerParams(dimension_semantics=("parallel",) * n_axes,
                                vmem_limit_bytes=VMEM_LIMIT)


def _layer_norm(z, g, b):
    mu = jnp.mean(z, axis=-1, keepdims=True)
    zc = z - mu
    var = jnp.mean(zc * zc, axis=-1, keepdims=True)
    return zc * lax.rsqrt(var + LN_EPS) * g + b


def _ffn_ln_body(x_ref, win_ref, wout_ref, g_ref, b_ref, o_ref):
    x = x_ref[...]
    xb = x.astype(BF16)
    hg = jnp.dot(xb, win_ref[:, :D_FF], preferred_element_type=F32)
    hu = jnp.dot(xb, win_ref[:, D_FF:], preferred_element_type=F32)
    h = (hg * jax.nn.sigmoid(hg) * hu).astype(BF16)
    y = jnp.dot(h, wout_ref[...], preferred_element_type=F32)
    o_ref[...] = _layer_norm(DEEPNORM_ALPHA * x + 0.5 * y, g_ref[...], b_ref[...])


def _ffn_ln(x2d, w_in, w_out, g, b):
    t = x2d.shape[0]
    return pl.pallas_call(
        _ffn_ln_body,
        grid=(t // TM_FFN,),
        in_specs=[
            pl.BlockSpec((TM_FFN, D_MODEL), lambda i: (i, 0)),
            _const_spec((D_MODEL, 2 * D_FF)),
            _const_spec((D_FF, D_MODEL)),
            _const_spec((1, D_MODEL)),
            _const_spec((1, D_MODEL)),
        ],
        out_specs=pl.BlockSpec((TM_FFN, D_MODEL), lambda i: (i, 0)),
        out_shape=jax.ShapeDtypeStruct((t, D_MODEL), F32),
        compiler_params=_params(1),
        name="ffn_ln",
    )(x2d, w_in, w_out, g, b)


def _mem_kv_body(m_ref, w_ref, k_ref, v_ref):
    kv = jnp.dot(m_ref[...].astype(BF16), w_ref[...], preferred_element_type=F32)
    k_ref[...] = kv[:, :MEM_WIDTH].astype(BF16)
    v_ref[...] = kv[:, MEM_WIDTH:].astype(BF16)


def _mem_kv(mem2d, w_kv, n_mem):
    rows = mem2d.shape[0]
    return pl.pallas_call(
        _mem_kv_body,
        grid=(rows // n_mem,),
        in_specs=[
            pl.BlockSpec((n_mem, D_MODEL), lambda i: (i, 0)),
            _const_spec((D_MODEL, 2 * MEM_WIDTH)),
        ],
        out_specs=[pl.BlockSpec((n_mem, MEM_WIDTH), lambda i: (i, 0))] * 2,
        out_shape=[jax.ShapeDtypeStruct((rows, MEM_WIDTH), BF16)] * 2,
        compiler_params=_params(1),
        name="mem_kv",
    )(mem2d, w_kv)


def _proj_body(x_ref, pos_ref, invf_ref, w_ref, gmg_ref, gmb_ref, ws_ref, bst_ref, km_ref, vm_ref,
               q_ref, k_ref, vt_ref, ygm_ref, ymem_ref):
    tm = x_ref.shape[0]
    xb = x_ref[...].astype(BF16)
    h = jnp.dot(xb, w_ref[...], preferred_element_type=F32)

    u = jax.nn.gelu(h[:, OFF_U:OFF_U + GM_WIDTH])
    v = jax.nn.gelu(h[:, OFF_V:OFF_V + GM_WIDTH])
    vn = _layer_norm(v, gmg_ref[...], gmb_ref[...]).astype(BF16)
    row = lax.broadcasted_iota(jnp.int32, (GM_CHUNK, GM_CHUNK), 0)
    col = lax.broadcasted_iota(jnp.int32, (GM_CHUNK, GM_CHUNK), 1)
    causal = col <= row
    for g in range(GM_GROUPS):
        wc = jnp.where(causal, ws_ref[g], 0.0).astype(BF16)
        bcol = bst_ref[:, g:g + 1]
        gs = slice(g * GM_GROUP_DIM, (g + 1) * GM_GROUP_DIM)
        for c in range(tm // GM_CHUNK):
            rs = slice(c * GM_CHUNK, (c + 1) * GM_CHUNK)
            mixed = jnp.dot(wc, vn[rs, gs], preferred_element_type=F32) + bcol
            ygm_ref[rs, gs] = (u[rs, gs] * mixed).astype(BF16)

    ang = pos_ref[...].astype(F32) * invf_ref[...]
    cos = jnp.cos(ang)
    sin = jnp.sin(ang)
    sub = lax.broadcasted_iota(jnp.int32, (tm, LANES), 1) % DIFF_HEAD_DIM
    sin_lo = jnp.where(sub < ROPE_HALF, -sin, 0.0)
    sin_hi = jnp.where(sub >= ROPE_HALF, sin, 0.0)

    def rope(t):
        return (t * cos + pltpu.roll(t, LANES - ROPE_HALF, axis=1) * sin_lo
                + pltpu.roll(t, ROPE_HALF, axis=1) * sin_hi)

    scale = DIFF_HEAD_DIM ** -0.5 * math.log2(math.e)
    for j in range(DIFF_Q_WIDTH // LANES):
        cs = slice(j * LANES, (j + 1) * LANES)
        q_ref[:, cs] = (rope(h[:, OFF_Q + j * LANES:OFF_Q + (j + 1) * LANES]) * scale).astype(BF16)
        k_ref[:, cs] = rope(h[:, OFF_K + j * LANES:OFF_K + (j + 1) * LANES]).astype(BF16)
    vt = h[:, OFF_VD:OFF_VD + DIFF_V_WIDTH].T.astype(BF16)
    ones = jnp.ones((VT_ROWS - DIFF_V_DIM, tm), BF16)
    for hh in range(DIFF_HEADS):
        vt_ref[0, hh * VT_ROWS:hh * VT_ROWS + DIFF_V_DIM] = vt[hh * DIFF_V_DIM:(hh + 1) * DIFF_V_DIM]
        vt_ref[0, hh * VT_ROWS + DIFF_V_DIM:(hh + 1) * VT_ROWS] = ones

    mscale = MEM_HEAD_DIM ** -0.5
    outs = []
    for hh in range(MEM_HEADS):
        hs = slice(hh * MEM_HEAD_DIM, (hh + 1) * MEM_HEAD_DIM)
        qh = (h[:, OFF_QM + hh * MEM_HEAD_DIM:OFF_QM + (hh + 1) * MEM_HEAD_DIM] * mscale).astype(BF16)
        s = lax.dot_general(qh, km_ref[:, hs], NT_DIMS, preferred_element_type=F32)
        e = jnp.exp(s - jnp.max(s, axis=-1, keepdims=True))
        p = e * (1.0 / jnp.sum(e, axis=-1, keepdims=True))
        outs.append(jnp.dot(p.astype(BF16), vm_ref[:, hs], preferred_element_type=F32))
    ymem_ref[...] = jnp.concatenate(outs, axis=-1).astype(BF16)


def _proj(x2d, pos2d, invf, w, gm_g, gm_b, w_s, b_st, km, vm, seq, n_mem):
    t = x2d.shape[0]
    tiles_per_batch = seq // TM_PROJ
    row_spec = lambda width: pl.BlockSpec((TM_PROJ, width), lambda i: (i, 0))
    mem_spec = pl.BlockSpec((n_mem, MEM_WIDTH), lambda i: (i // tiles_per_batch, 0))
    return pl.pallas_call(
        _proj_body,
        grid=(t // TM_PROJ,),
        in_specs=[
            row_spec(D_MODEL),
            row_spec(1),
            _const_spec((1, LANES)),
            _const_spec((D_MODEL, OFF_GATE)),
            _const_spec((1, GM_WIDTH)),
            _const_spec((1, GM_WIDTH)),
            _const_spec((GM_GROUPS, GM_CHUNK, GM_CHUNK)),
            _const_spec((GM_CHUNK, GM_GROUPS)),
            mem_spec,
            mem_spec,
        ],
        out_specs=[row_spec(DIFF_Q_WIDTH), row_spec(DIFF_Q_WIDTH),
                   pl.BlockSpec((1, DIFF_HEADS * VT_ROWS, TM_PROJ),
                                lambda i: (i // tiles_per_batch, 0, i % tiles_per_batch)),
                   row_spec(GM_WIDTH), row_spec(MEM_WIDTH)],
        out_shape=[jax.ShapeDtypeStruct((t, DIFF_Q_WIDTH), BF16),
                   jax.ShapeDtypeStruct((t, DIFF_Q_WIDTH), BF16),
                   jax.ShapeDtypeStruct((t // seq, DIFF_HEADS * VT_ROWS, seq), BF16),
                   jax.ShapeDtypeStruct((t, GM_WIDTH), BF16),
                   jax.ShapeDtypeStruct((t, MEM_WIDTH), BF16)],
        compiler_params=_params(1),
        name="proj",
    )(x2d, pos2d, invf, w, gm_g, gm_b, w_s, b_st, km, vm)


def _diff_attn_body(lamv_ref, ngc_ref, q_ref, k_ref, vt_ref, o_ref, qs_ref, st_ref, e_ref, acc_ref):
    qi = pl.program_id(1)
    heads = range(DIFF_HEADS)
    lane = lax.broadcasted_iota(jnp.int32, (TQ, DIFF_V_DIM), 1)
    for h in heads:
        q = q_ref[0, :, h * DIFF_V_DIM:(h + 1) * DIFF_V_DIM]
        zero = jnp.zeros_like(q)
        qs_ref[h, :TQ] = jnp.where(lane < DIFF_HEAD_DIM, q, zero)
        qs_ref[h, TQ:] = jnp.where(lane >= DIFF_HEAD_DIM, q, zero)
    acc_ref[...] = jnp.zeros(acc_ref.shape, F32)
    e_ref[...] = jnp.zeros(e_ref.shape, BF16)

    def scores(h, ki):
        kb = k_ref[0, pl.ds(pl.multiple_of(ki * TQ, TQ), TQ), h * DIFF_V_DIM:(h + 1) * DIFF_V_DIM]
        st = lax.dot_general(kb, qs_ref[h], NT_DIMS, preferred_element_type=F32)
        st_ref[h] = st
        return jnp.max(st, axis=0, keepdims=True)

    def values(h, ki):
        vtb = vt_ref[0, h * VT_ROWS:(h + 1) * VT_ROWS, pl.ds(pl.multiple_of(ki * TQ, TQ), TQ)]
        return jnp.dot(vtb, e_ref[h], preferred_element_type=F32)

    def softmax(h, st, m, bm):
        m_new = jnp.maximum(m, bm)
        e_ref[h] = jnp.exp2(st - m_new).astype(BF16)
        return jnp.exp2(m - m_new), m_new

    def step(t, carry):
        a_prev, m, bm = carry
        pv = [values(h, jnp.maximum(t - 1, 0)) for h in heads]
        am = [softmax(h, st_ref[h], m[h], bm[h]) for h in heads]
        for h in heads:
            acc_ref[h] = a_prev[h] * acc_ref[h] + pv[h]
        return (tuple(x[0] for x in am), tuple(x[1] for x in am),
                tuple(scores(h, t + 1) for h in heads))

    init = (tuple(jnp.ones((1, 2 * TQ), F32) for _ in heads),
            tuple(jnp.full((1, 2 * TQ), -jnp.inf, F32) for _ in heads),
            tuple(scores(h, 0) for h in heads))
    a_prev, m, _ = lax.fori_loop(0, qi, step, init)

    lv = lamv_ref[...]
    lam = (jnp.exp(jnp.sum(lv[0:1] * lv[1:2], axis=-1, keepdims=True))
           - jnp.exp(jnp.sum(lv[2:3] * lv[3:4], axis=-1, keepdims=True)) + LAM_INIT)
    kpos = lax.broadcasted_iota(jnp.int32, (TQ, 2 * TQ), 0)
    qpos = lax.broadcasted_iota(jnp.int32, (TQ, 2 * TQ), 1) % TQ
    visible = kpos <= qpos
    pv = [values(h, jnp.maximum(qi - 1, 0)) for h in heads]
    for h in heads:
        st = jnp.where(visible, st_ref[h], -jnp.inf)
        a, _ = softmax(h, st, m[h], jnp.max(st, axis=0, keepdims=True))
        acc = a * (a_prev[h] * acc_ref[h] + pv[h]) + values(h, qi)
        on = acc[:DIFF_V_DIM] * (1.0 / acc[DIFF_V_DIM:DIFF_V_DIM + 1])
        ot = on[:, :TQ] - lam * on[:, TQ:]
        ot = ot * lax.rsqrt(jnp.mean(ot * ot, axis=0, keepdims=True) + LN_EPS) * ngc_ref[...]
        o_ref[0, :, h * DIFF_V_DIM:(h + 1) * DIFF_V_DIM] = (ot * (1.0 - LAM_INIT)).T.astype(BF16)


def _diff_attn(lamv, norm_g_col, q, k, vt):
    b, s, _ = q.shape
    return pl.pallas_call(
        _diff_attn_body,
        grid=(b, s // TQ),
        in_specs=[
            _const_spec((4, DIFF_HEAD_DIM)),
            _const_spec((DIFF_V_DIM, 1)),
            pl.BlockSpec((1, TQ, DIFF_Q_WIDTH), lambda bi, qi: (bi, qi, 0)),
            pl.BlockSpec((1, s, DIFF_Q_WIDTH), lambda bi, qi: (bi, 0, 0)),
            pl.BlockSpec((1, DIFF_HEADS * VT_ROWS, s), lambda bi, qi: (bi, 0, 0)),
        ],
        out_specs=pl.BlockSpec((1, TQ, DIFF_V_WIDTH), lambda bi, qi: (bi, qi, 0)),
        out_shape=jax.ShapeDtypeStruct((b, s, DIFF_V_WIDTH), BF16),
        scratch_shapes=[pltpu.VMEM((DIFF_HEADS, 2 * TQ, DIFF_V_DIM), BF16),
                        pltpu.VMEM((DIFF_HEADS, TQ, 2 * TQ), F32),
                        pltpu.VMEM((DIFF_HEADS, TQ, 2 * TQ), BF16),
                        pltpu.VMEM((DIFF_HEADS, VT_ROWS, 2 * TQ), F32)],
        compiler_params=_params(2),
        name="diff_attn",
    )(lamv, norm_g_col, q, k, vt)


def _merge_body(x_ref, ygm_ref, ydf_ref, ymm_ref, wgate_ref, gb_ref, wgm_ref, wdf_ref, wmm_ref, wo_ref,
                g_ref, b_ref, o_ref):
    x = x_ref[...]
    gates = jax.nn.sigmoid(jnp.dot(x.astype(BF16), wgate_ref[...], preferred_element_type=F32) + gb_ref[...])
    merged = (gates[:, :D_MODEL] * jnp.dot(ygm_ref[...], wgm_ref[...], preferred_element_type=F32)
              + gates[:, D_MODEL:2 * D_MODEL] * jnp.dot(ydf_ref[...], wdf_ref[...], preferred_element_type=F32)
              + gates[:, 2 * D_MODEL:] * jnp.dot(ymm_ref[...], wmm_ref[...], preferred_element_type=F32))
    y = jnp.dot(merged.astype(BF16), wo_ref[...], preferred_element_type=F32)
    o_ref[...] = _layer_norm(DEEPNORM_ALPHA * x + y, g_ref[...], b_ref[...])


def _merge(x2d, ygm, ydf, ymm, w_gate, gate_b, w_gm, w_df, w_mm, w_o, g, b):
    t = x2d.shape[0]
    row_spec = lambda width: pl.BlockSpec((TM_MERGE, width), lambda i: (i, 0))
    return pl.pallas_call(
        _merge_body,
        grid=(t // TM_MERGE,),
        in_specs=[
            row_spec(D_MODEL), row_spec(GM_WIDTH), row_spec(DIFF_V_WIDTH), row_spec(MEM_WIDTH),
            _const_spec((D_MODEL, N_BRANCH * D_MODEL)),
            _const_spec((1, N_BRANCH * D_MODEL)),
            _const_spec((GM_WIDTH, D_MODEL)),
            _const_spec((DIFF_V_WIDTH, D_MODEL)),
            _const_spec((MEM_WIDTH, D_MODEL)),
            _const_spec((D_MODEL, D_MODEL)),
            _const_spec((1, D_MODEL)),
            _const_spec((1, D_MODEL)),
        ],
        out_specs=row_spec(D_MODEL),
        out_shape=jax.ShapeDtypeStruct((t, D_MODEL), F32),
        compiler_params=_params(1),
        name="merge",
    )(x2d, ygm, ydf, ymm, w_gate, gate_b, w_gm, w_df, w_mm, w_o, g, b)


def _rope_lane_freqs():
    inv_freq = ROPE_THETA ** (-jnp.arange(0, ROPE_DIM, 2, dtype=F32) / ROPE_DIM)
    sub = jnp.arange(LANES) % DIFF_HEAD_DIM
    return jnp.where(sub < ROPE_DIM, inv_freq[sub % ROPE_HALF], 0.0).astype(F32).reshape(1, LANES)


def kernel(x, mem, positions, ffn1_w_in, ffn1_w_out, ln1_g, ln1_b, w_in, gate_b, gm_ln_g, gm_ln_b, gm_w_s, gm_b_s,
           lambda_q1, lambda_k1, lambda_q2, lambda_k2, diff_norm_g, w_mem_kv, w_branch_gm, w_branch_diff,
           w_branch_mem, w_o, ln2_g, ln2_b, ffn2_w_in, ffn2_w_out, ln3_g, ln3_b):
    bsz, seq, _ = x.shape
    n_mem = mem.shape[1]
    t = bsz * seq
    i = 0
    x2d = x.reshape(t, D_MODEL)

    x1 = _ffn_ln(x2d, ffn1_w_in[i].astype(BF16), ffn1_w_out[i].astype(BF16), ln1_g[i][None], ln1_b[i][None])

    km, vm = _mem_kv(mem.reshape(bsz * n_mem, D_MODEL), w_mem_kv[i].astype(BF16), n_mem)
    q, k, vt, ygm, ymm = _proj(
        x1, positions.reshape(t, 1), _rope_lane_freqs(), w_in[i][:, :OFF_GATE].astype(BF16),
        gm_ln_g[i][None], gm_ln_b[i][None], gm_w_s[i], gm_b_s[i].T, km, vm, seq, n_mem)

    lamv = jnp.stack([lambda_q1[i], lambda_k1[i], lambda_q2[i], lambda_k2[i]]).astype(F32)
    ydf = _diff_attn(lamv, diff_norm_g[i][:, None],
                     q.reshape(bsz, seq, DIFF_Q_WIDTH), k.reshape(bsz, seq, DIFF_Q_WIDTH),
                     vt).reshape(t, DIFF_V_WIDTH)

    x2 = _merge(x1, ygm, ydf, ymm, w_in[i][:, OFF_GATE:].astype(BF16), gate_b[i][None],
                w_branch_gm[i].astype(BF16), w_branch_diff[i].astype(BF16), w_branch_mem[i].astype(BF16),
                w_o[i].astype(BF16), ln2_g[i][None], ln2_b[i][None])

    x3 = _ffn_ln(x2, ffn2_w_in[i].astype(BF16), ffn2_w_out[i].astype(BF16), ln3_g[i][None], ln3_b[i][None])
    return x3.reshape(bsz, seq, D_MODEL)
```

```python
import functools
import math

import jax
import jax.numpy as jnp
from jax import lax
from jax.experimental import pallas as pl
from jax.experimental.pallas import tpu as pltpu

F32 = jnp.float32
BF16 = jnp.bfloat16

D_MODEL = 1024
D_FF = 2816
GM_WIDTH = 512
GM_GROUPS = 4
GM_GROUP_DIM = GM_WIDTH // GM_GROUPS
GM_CHUNK = 128
DIFF_HEADS = 4
DIFF_HEAD_DIM = 64
DIFF_V_DIM = 2 * DIFF_HEAD_DIM
DIFF_Q_WIDTH = DIFF_HEADS * 2 * DIFF_HEAD_DIM
DIFF_V_WIDTH = DIFF_HEADS * DIFF_V_DIM
MEM_HEADS = 4
MEM_HEAD_DIM = 64
MEM_WIDTH = MEM_HEADS * MEM_HEAD_DIM
N_BRANCH = 3
ROPE_THETA = 500000.0
ROPE_DIM = DIFF_HEAD_DIM // 4
ROPE_HALF = ROPE_DIM // 2
DEPTH = 1
DEEPNORM_ALPHA = (2 * DEPTH) ** 0.25
LN_EPS = 1e-5
LAM_INIT = 0.8 - 0.6 * math.exp(-0.3 * 0)
OFF_U = 0
OFF_V = OFF_U + GM_WIDTH
OFF_Q = OFF_V + GM_WIDTH
OFF_K = OFF_Q + DIFF_Q_WIDTH
OFF_VD = OFF_K + DIFF_Q_WIDTH
OFF_QM = OFF_VD + DIFF_V_WIDTH
OFF_GATE = OFF_QM + MEM_WIDTH

LANES = 128
BF16_SUBLANES = 16
VT_ROWS = DIFF_V_DIM + BF16_SUBLANES
MEM_VT_ROWS = MEM_HEAD_DIM + BF16_SUBLANES
MXU_COLS = 256
VMEM_LIMIT = 52 * 1024 * 1024

TM_FFN = 512
TM_PROJ = 256
TM_MERGE = 512
TQ = 512

NT_DIMS = (((1,), (1,)), ((), ()))


def _const_spec(shape):
    nd = len(shape)
    return pl.BlockSpec(shape, lambda *_: (0,) * nd, pipeline_mode=pl.Buffered(1))


def _params(n_axes):
    return pltpu.CompilerParams(dimension_semantics=("parallel",) * n_axes,
                                vmem_limit_bytes=VMEM_LIMIT)


def _params_sequential():
    return pltpu.CompilerParams(dimension_semantics=("arbitrary",), vmem_limit_bytes=VMEM_LIMIT)


def _sigmoid(x):
    return 0.5 * jnp.tanh(0.5 * x) + 0.5


def _layer_norm(z, g, b):
    mu = jnp.mean(z, axis=-1, keepdims=True)
    zc = z - mu
    var = jnp.mean(zc * zc, axis=-1, keepdims=True)
    return zc * lax.rsqrt(var + LN_EPS) * g + b


def _lagged_row_specs(n_tiles, tm):
    in_spec = lambda width: pl.BlockSpec((tm, width), lambda i: (jnp.minimum(i, n_tiles - 1), 0))
    out_spec = lambda width: pl.BlockSpec((tm, width), lambda i: (jnp.maximum(i - 1, 0), 0))
    return in_spec, out_spec


def _lagged_norm(z_ref, g_ref, b_ref, o_ref):
    n_groups = z_ref.shape[1] // MXU_COLS
    rows = z_ref.shape[0] // n_groups
    zero_rows = []
    for j in range(n_groups):
        rs = slice(j * rows, (j + 1) * rows)
        out = _layer_norm(z_ref[rs, :], g_ref[...], b_ref[...])
        o_ref[rs, :] = out
        bits = lax.bitcast_convert_type(jnp.max(out, axis=0, keepdims=True)[:, :MXU_COLS], jnp.uint32)
        zero_rows.append(lax.bitcast_convert_type((bits >> 16) >> 16, F32))
    return zero_rows


def _run_lagged(tile_fn, z_ref, g_ref, b_ref, o_ref):
    step = pl.program_id(0)
    last = pl.num_programs(0) - 1

    @pl.when(step == 0)
    def _():
        z_ref[...] = jnp.zeros(z_ref.shape, F32)

    @pl.when(step < last)
    def _():
        zero_rows = _lagged_norm(z_ref, g_ref, b_ref, o_ref)
        z_ref[...] = tile_fn(zero_rows)

    @pl.when(step == last)
    def _():
        _lagged_norm(z_ref, g_ref, b_ref, o_ref)


def _ffn_ln_body(x_ref, win_ref, wout_ref, g_ref, b_ref, o_ref, z_ref, h_ref):
    def tile(zero_rows):
        x = x_ref[...]
        xb = x.astype(BF16)
        hg = jnp.dot(xb, win_ref[:, :D_FF], preferred_element_type=F32)
        hu = jnp.dot(xb, win_ref[:, D_FF:], preferred_element_type=F32)
        for c in range(D_FF // MXU_COLS):
            cs = slice(c * MXU_COLS, (c + 1) * MXU_COLS)
            g = hg[:, cs] + zero_rows[c] if c < len(zero_rows) else hg[:, cs]
            h_ref[:, cs] = (g * _sigmoid(g) * hu[:, cs]).astype(BF16)
        y = jnp.dot(h_ref[...], wout_ref[...], preferred_element_type=F32)
        return DEEPNORM_ALPHA * x + 0.5 * y

    _run_lagged(tile, z_ref, g_ref, b_ref, o_ref)


def _ffn_ln(x2d, w_in, w_out, g, b):
    t = x2d.shape[0]
    n_tiles = t // TM_FFN
    in_spec, out_spec = _lagged_row_specs(n_tiles, TM_FFN)
    return pl.pallas_call(
        _ffn_ln_body,
        grid=(n_tiles + 1,),
        in_specs=[
            in_spec(D_MODEL),
            _const_spec((D_MODEL, 2 * D_FF)),
            _const_spec((D_FF, D_MODEL)),
            _const_spec((1, D_MODEL)),
            _const_spec((1, D_MODEL)),
        ],
        out_specs=out_spec(D_MODEL),
        out_shape=jax.ShapeDtypeStruct((t, D_MODEL), F32),
        scratch_shapes=[pltpu.VMEM((TM_FFN, D_MODEL), F32), pltpu.VMEM((TM_FFN, D_FF), BF16)],
        compiler_params=_params_sequential(),
        name="ffn_ln",
    )(x2d, w_in, w_out, g, b)


def _mem_kv_body(m_ref, w_ref, km_ref, vmt_ref):
    n_mem = m_ref.shape[0]
    kv = jnp.dot(m_ref[...].astype(BF16), w_ref[...], preferred_element_type=F32)
    k = kv[:, :MEM_WIDTH].astype(BF16)
    vt = kv[:, MEM_WIDTH:].T.astype(BF16)
    lane = lax.broadcasted_iota(jnp.int32, k.shape, 1)
    ones = jnp.ones((MEM_VT_ROWS - MEM_HEAD_DIM, n_mem), BF16)
    for hh in range(MEM_HEADS):
        km_ref[0, hh] = jnp.where(lane // MEM_HEAD_DIM == hh, k, jnp.zeros_like(k))
        vmt_ref[0, hh * MEM_VT_ROWS:hh * MEM_VT_ROWS + MEM_HEAD_DIM] = vt[hh * MEM_HEAD_DIM:(hh + 1) * MEM_HEAD_DIM]
        vmt_ref[0, hh * MEM_VT_ROWS + MEM_HEAD_DIM:(hh + 1) * MEM_VT_ROWS] = ones


def _mem_kv(mem2d, w_kv, n_mem):
    bsz = mem2d.shape[0] // n_mem
    return pl.pallas_call(
        _mem_kv_body,
        grid=(bsz,),
        in_specs=[
            pl.BlockSpec((n_mem, D_MODEL), lambda i: (i, 0)),
            _const_spec((D_MODEL, 2 * MEM_WIDTH)),
        ],
        out_specs=[pl.BlockSpec((1, MEM_HEADS, n_mem, MEM_WIDTH), lambda i: (i, 0, 0, 0)),
                   pl.BlockSpec((1, MEM_HEADS * MEM_VT_ROWS, n_mem), lambda i: (i, 0, 0))],
        out_shape=[jax.ShapeDtypeStruct((bsz, MEM_HEADS, n_mem, MEM_WIDTH), BF16),
                   jax.ShapeDtypeStruct((bsz, MEM_HEADS * MEM_VT_ROWS, n_mem), BF16)],
        compiler_params=_params(1),
        name="mem_kv",
    )(mem2d, w_kv)


def _proj_body(x_ref, pos_ref, invf_ref, w_ref, gmg_ref, gmb_ref, ws_ref, bst_ref, km_ref, vmt_ref,
               q_ref, k_ref, vt_ref, ygm_ref, ymem_ref):
    tm = x_ref.shape[0]
    xb = x_ref[...].astype(BF16)
    h = jnp.dot(xb, w_ref[...], preferred_element_type=F32)

    qm = (h[:, OFF_QM:OFF_QM + MEM_WIDTH] * (MEM_HEAD_DIM ** -0.5 * math.log2(math.e))).astype(BF16)
    st = [lax.dot_general(km_ref[0, hh], qm, NT_DIMS, preferred_element_type=F32) for hh in range(MEM_HEADS)]
    e = [jnp.exp2(s - jnp.max(s, axis=0, keepdims=True)).astype(BF16) for s in st]
    outs = []
    for hh in range(MEM_HEADS):
        ot = jnp.dot(vmt_ref[0, hh * MEM_VT_ROWS:(hh + 1) * MEM_VT_ROWS], e[hh], preferred_element_type=F32)
        outs.append(ot[:MEM_HEAD_DIM] * (1.0 / ot[MEM_HEAD_DIM:MEM_HEAD_DIM + 1]))
    ymem_ref[...] = jnp.concatenate(outs, axis=0).T.astype(BF16)

    u = jax.nn.gelu(h[:, OFF_U:OFF_U + GM_WIDTH])
    v = jax.nn.gelu(h[:, OFF_V:OFF_V + GM_WIDTH])
    vn = _layer_norm(v, gmg_ref[...], gmb_ref[...]).astype(BF16)
    row = lax.broadcasted_iota(jnp.int32, (GM_CHUNK, GM_CHUNK), 0)
    col = lax.broadcasted_iota(jnp.int32, (GM_CHUNK, GM_CHUNK), 1)
    causal = col <= row
    for g in range(GM_GROUPS):
        wc = jnp.where(causal, ws_ref[g], 0.0).astype(BF16)
        bcol = bst_ref[:, g:g + 1]
        gs = slice(g * GM_GROUP_DIM, (g + 1) * GM_GROUP_DIM)
        for c in range(tm // GM_CHUNK):
            rs = slice(c * GM_CHUNK, (c + 1) * GM_CHUNK)
            mixed = jnp.dot(wc, vn[rs, gs], preferred_element_type=F32) + bcol
            ygm_ref[rs, gs] = (u[rs, gs] * mixed).astype(BF16)

    ang = invf_ref[...] * pos_ref[0].astype(F32)
    cos_t = jnp.cos(ang)
    sin_t = jnp.sin(ang)
    rest = DIFF_HEAD_DIM - ROPE_DIM
    one_t = jnp.ones((rest, tm), F32)
    zero_t = jnp.zeros((rest, tm), F32)
    zero_h = jnp.zeros((ROPE_HALF, tm), F32)
    cos = jnp.concatenate([cos_t, cos_t, one_t] * 2, axis=0).T
    sin_lo = jnp.concatenate([-sin_t, zero_h, zero_t] * 2, axis=0).T
    sin_hi = jnp.concatenate([zero_h, sin_t, zero_t] * 2, axis=0).T

    def rope(t):
        return (t * cos + pltpu.roll(t, LANES - ROPE_HALF, axis=1) * sin_lo
                + pltpu.roll(t, ROPE_HALF, axis=1) * sin_hi)

    scale = DIFF_HEAD_DIM ** -0.5 * math.log2(math.e)
    for j in range(DIFF_Q_WIDTH // LANES):
        cs = slice(j * LANES, (j + 1) * LANES)
        q_ref[:, cs] = (rope(h[:, OFF_Q + j * LANES:OFF_Q + (j + 1) * LANES]) * scale).astype(BF16)
        k_ref[:, cs] = rope(h[:, OFF_K + j * LANES:OFF_K + (j + 1) * LANES]).astype(BF16)
    vt = h[:, OFF_VD:OFF_VD + DIFF_V_WIDTH].T.astype(BF16)
    ones = jnp.ones((VT_ROWS - DIFF_V_DIM, tm), BF16)
    for hh in range(DIFF_HEADS):
        vt_ref[0, hh * VT_ROWS:hh * VT_ROWS + DIFF_V_DIM] = vt[hh * DIFF_V_DIM:(hh + 1) * DIFF_V_DIM]
        vt_ref[0, hh * VT_ROWS + DIFF_V_DIM:(hh + 1) * VT_ROWS] = ones


def _proj(x2d, pos3d, invf, w, gm_g, gm_b, w_s, b_st, km, vmt, seq, n_mem):
    t = x2d.shape[0]
    tiles_per_batch = seq // TM_PROJ
    row_spec = lambda width: pl.BlockSpec((TM_PROJ, width), lambda i: (i, 0))
    return pl.pallas_call(
        _proj_body,
        grid=(t // TM_PROJ,),
        in_specs=[
            row_spec(D_MODEL),
            pl.BlockSpec((1, 1, TM_PROJ), lambda i: (i, 0, 0)),
            _const_spec((ROPE_HALF, 1)),
            _const_spec((D_MODEL, OFF_GATE)),
            _const_spec((1, GM_WIDTH)),
            _const_spec((1, GM_WIDTH)),
            _const_spec((GM_GROUPS, GM_CHUNK, GM_CHUNK)),
            _const_spec((GM_CHUNK, GM_GROUPS)),
            pl.BlockSpec((1, MEM_HEADS, n_mem, MEM_WIDTH), lambda i: (i // tiles_per_batch, 0, 0, 0)),
            pl.BlockSpec((1, MEM_HEADS * MEM_VT_ROWS, n_mem), lambda i: (i // tiles_per_batch, 0, 0)),
        ],
        out_specs=[row_spec(DIFF_Q_WIDTH), row_spec(DIFF_Q_WIDTH),
                   pl.BlockSpec((1, DIFF_HEADS * VT_ROWS, TM_PROJ),
                                lambda i: (i // tiles_per_batch, 0, i % tiles_per_batch)),
                   row_spec(GM_WIDTH), row_spec(MEM_WIDTH)],
        out_shape=[jax.ShapeDtypeStruct((t, DIFF_Q_WIDTH), BF16),
                   jax.ShapeDtypeStruct((t, DIFF_Q_WIDTH), BF16),
                   jax.ShapeDtypeStruct((t // seq, DIFF_HEADS * VT_ROWS, seq), BF16),
                   jax.ShapeDtypeStruct((t, GM_WIDTH), BF16),
                   jax.ShapeDtypeStruct((t, MEM_WIDTH), BF16)],
        compiler_params=_params(1),
        name="proj",
    )(x2d, pos3d, invf, w, gm_g, gm_b, w_s, b_st, km, vmt)


def _diff_attn_body(lamv_ref, ngc_ref, q_ref, k_ref, vt_ref, o_ref, qs_ref, st_ref, e_ref, acc_ref):
    qi = pl.program_id(1)
    heads = range(DIFF_HEADS)
    lane = lax.broadcasted_iota(jnp.int32, (TQ, DIFF_V_DIM), 1)
    for h in heads:
        q = q_ref[0, :, h * DIFF_V_DIM:(h + 1) * DIFF_V_DIM]
        zero = jnp.zeros_like(q)
        qs_ref[h, :TQ] = jnp.where(lane < DIFF_HEAD_DIM, q, zero)
        qs_ref[h, TQ:] = jnp.where(lane >= DIFF_HEAD_DIM, q, zero)
    acc_ref[...] = jnp.zeros(acc_ref.shape, F32)
    e_ref[...] = jnp.zeros(e_ref.shape, BF16)

    def scores(h, ki):
        kb = k_ref[0, pl.ds(pl.multiple_of(ki * TQ, TQ), TQ), h * DIFF_V_DIM:(h + 1) * DIFF_V_DIM]
        st = lax.dot_general(kb, qs_ref[h], NT_DIMS, preferred_element_type=F32)
        st_ref[h] = st
        return jnp.max(st, axis=0, keepdims=True)

    def values(h, ki):
        vtb = vt_ref[0, h * VT_ROWS:(h + 1) * VT_ROWS, pl.ds(pl.multiple_of(ki * TQ, TQ), TQ)]
        return jnp.dot(vtb, e_ref[h], preferred_element_type=F32)

    def softmax(h, st, m, bm):
        m_new = jnp.maximum(m, bm)
        e_ref[h] = jnp.exp2(st - m_new).astype(BF16)
        return jnp.exp2(m - m_new), m_new

    def step(t, carry):
        a_prev, m, bm = carry
        pv = [values(h, jnp.maximum(t - 1, 0)) for h in heads]
        am = [softmax(h, st_ref[h], m[h], bm[h]) for h in heads]
        for h in heads:
            acc_ref[h] = a_prev[h] * acc_ref[h] + pv[h]
        return (tuple(x[0] for x in am), tuple(x[1] for x in am),
                tuple(scores(h, t + 1) for h in heads))

    init = (tuple(jnp.ones((1, 2 * TQ), F32) for _ in heads),
            tuple(jnp.full((1, 2 * TQ), -jnp.inf, F32) for _ in heads),
            tuple(scores(h, 0) for h in heads))
    a_prev, m, _ = lax.fori_loop(0, qi, step, init)

    lv = lamv_ref[...]
    lam = (jnp.exp(jnp.sum(lv[0:1] * lv[1:2], axis=-1, keepdims=True))
           - jnp.exp(jnp.sum(lv[2:3] * lv[3:4], axis=-1, keepdims=True)) + LAM_INIT)
    kpos = lax.broadcasted_iota(jnp.int32, (TQ, 2 * TQ), 0)
    qpos = lax.broadcasted_iota(jnp.int32, (TQ, 2 * TQ), 1) % TQ
    visible = kpos <= qpos
    pv = [values(h, jnp.maximum(qi - 1, 0)) for h in heads]
    for h in heads:
        st = jnp.where(visible, st_ref[h], -jnp.inf)
        a, _ = softmax(h, st, m[h], jnp.max(st, axis=0, keepdims=True))
        acc = a * (a_prev[h] * acc_ref[h] + pv[h]) + values(h, qi)
        on = acc[:DIFF_V_DIM] * (1.0 / acc[DIFF_V_DIM:DIFF_V_DIM + 1])
        ot = on[:, :TQ] - lam * on[:, TQ:]
        ot = ot * lax.rsqrt(jnp.mean(ot * ot, axis=0, keepdims=True) + LN_EPS) * ngc_ref[...]
        o_ref[0, :, h * DIFF_V_DIM:(h + 1) * DIFF_V_DIM] = (ot * (1.0 - LAM_INIT)).T.astype(BF16)


def _diff_attn(lamv, norm_g_col, q, k, vt):
    b, s, _ = q.shape
    return pl.pallas_call(
        _diff_attn_body,
        grid=(b, s // TQ),
        in_specs=[
            _const_spec((4, DIFF_HEAD_DIM)),
            _const_spec((DIFF_V_DIM, 1)),
            pl.BlockSpec((1, TQ, DIFF_Q_WIDTH), lambda bi, qi: (bi, qi, 0)),
            pl.BlockSpec((1, s, DIFF_Q_WIDTH), lambda bi, qi: (bi, 0, 0)),
            pl.BlockSpec((1, DIFF_HEADS * VT_ROWS, s), lambda bi, qi: (bi, 0, 0)),
        ],
        out_specs=pl.BlockSpec((1, TQ, DIFF_V_WIDTH), lambda bi, qi: (bi, qi, 0)),
        out_shape=jax.ShapeDtypeStruct((b, s, DIFF_V_WIDTH), BF16),
        scratch_shapes=[pltpu.VMEM((DIFF_HEADS, 2 * TQ, DIFF_V_DIM), BF16),
                        pltpu.VMEM((DIFF_HEADS, TQ, 2 * TQ), F32),
                        pltpu.VMEM((DIFF_HEADS, TQ, 2 * TQ), BF16),
                        pltpu.VMEM((DIFF_HEADS, VT_ROWS, 2 * TQ), F32)],
        compiler_params=_params(2),
        name="diff_attn",
    )(lamv, norm_g_col, q, k, vt)


def _merge_body(x_ref, ygm_ref, ydf_ref, ymm_ref, wgate_ref, gb_ref, wgm_ref, wdf_ref, wmm_ref, wo_ref,
                g_ref, b_ref, o_ref, z_ref, m_ref):
    def tile(zero_rows):
        x = x_ref[...]
        logits = jnp.dot(x.astype(BF16), wgate_ref[...], preferred_element_type=F32)
        branches = (jnp.dot(ygm_ref[...], wgm_ref[...], preferred_element_type=F32),
                    jnp.dot(ydf_ref[...], wdf_ref[...], preferred_element_type=F32),
                    jnp.dot(ymm_ref[...], wmm_ref[...], preferred_element_type=F32))
        for c in range(D_MODEL // MXU_COLS):
            merged = zero_rows[c]
            for r in range(N_BRANCH):
                cs = slice(r * D_MODEL + c * MXU_COLS, r * D_MODEL + (c + 1) * MXU_COLS)
                gate = _sigmoid(logits[:, cs] + gb_ref[:, cs])
                merged = merged + gate * branches[r][:, c * MXU_COLS:(c + 1) * MXU_COLS]
            m_ref[:, c * MXU_COLS:(c + 1) * MXU_COLS] = merged.astype(BF16)
        y = jnp.dot(m_ref[...], wo_ref[...], preferred_element_type=F32)
        return DEEPNORM_ALPHA * x + y

    _run_lagged(tile, z_ref, g_ref, b_ref, o_ref)


def _merge(x2d, ygm, ydf, ymm, w_gate, gate_b, w_gm, w_df, w_mm, w_o, g, b):
    t = x2d.shape[0]
    n_tiles = t // TM_MERGE
    row_spec, out_spec = _lagged_row_specs(n_tiles, TM_MERGE)
    return pl.pallas_call(
        _merge_body,
        grid=(n_tiles + 1,),
        in_specs=[
            row_spec(D_MODEL), row_spec(GM_WIDTH), row_spec(DIFF_V_WIDTH), row_spec(MEM_WIDTH),
            _const_spec((D_MODEL, N_BRANCH * D_MODEL)),
            _const_spec((1, N_BRANCH * D_MODEL)),
            _const_spec((GM_WIDTH, D_MODEL)),
            _const_spec((DIFF_V_WIDTH, D_MODEL)),
            _const_spec((MEM_WIDTH, D_MODEL)),
            _const_spec((D_MODEL, D_MODEL)),
            _const_spec((1, D_MODEL)),
            _const_spec((1, D_MODEL)),
        ],
        out_specs=out_spec(D_MODEL),
        out_shape=jax.ShapeDtypeStruct((t, D_MODEL), F32),
        scratch_shapes=[pltpu.VMEM((TM_MERGE, D_MODEL), F32), pltpu.VMEM((TM_MERGE, D_MODEL), BF16)],
        compiler_params=_params_sequential(),
        name="merge",
    )(x2d, ygm, ydf, ymm, w_gate, gate_b, w_gm, w_df, w_mm, w_o, g, b)


def _rope_freqs():
    return (ROPE_THETA ** (-jnp.arange(0, ROPE_DIM, 2, dtype=F32) / ROPE_DIM)).reshape(ROPE_HALF, 1)


def kernel(x, mem, positions, ffn1_w_in, ffn1_w_out, ln1_g, ln1_b, w_in, gate_b, gm_ln_g, gm_ln_b, gm_w_s, gm_b_s,
           lambda_q1, lambda_k1, lambda_q2, lambda_k2, diff_norm_g, w_mem_kv, w_branch_gm, w_branch_diff,
           w_branch_mem, w_o, ln2_g, ln2_b, ffn2_w_in, ffn2_w_out, ln3_g, ln3_b):
    bsz, seq, _ = x.shape
    n_mem = mem.shape[1]
    t = bsz * seq
    i = 0
    x2d = x.reshape(t, D_MODEL)

    x1 = _ffn_ln(x2d, ffn1_w_in[i].astype(BF16), ffn1_w_out[i].astype(BF16), ln1_g[i][None], ln1_b[i][None])

    km, vmt = _mem_kv(mem.reshape(bsz * n_mem, D_MODEL), w_mem_kv[i].astype(BF16), n_mem)
    q, k, vt, ygm, ymm = _proj(
        x1, positions.reshape(t // TM_PROJ, 1, TM_PROJ), _rope_freqs(), w_in[i][:, :OFF_GATE].astype(BF16),
        gm_ln_g[i][None], gm_ln_b[i][None], gm_w_s[i], gm_b_s[i].T, km, vmt, seq, n_mem)

    lamv = jnp.stack([lambda_q1[i], lambda_k1[i], lambda_q2[i], lambda_k2[i]]).astype(F32)
    ydf = _diff_attn(lamv, diff_norm_g[i][:, None],
                     q.reshape(bsz, seq, DIFF_Q_WIDTH), k.reshape(bsz, seq, DIFF_Q_WIDTH),
                     vt).reshape(t, DIFF_V_WIDTH)

    x2 = _merge(x1, ygm, ydf, ymm, w_in[i][:, OFF_GATE:].astype(BF16), gate_b[i][None],
                w_branch_gm[i].astype(BF16), w_branch_diff[i].astype(BF16), w_branch_mem[i].astype(BF16),
                w_o[i].astype(BF16), ln2_g[i][None], ln2_b[i][None])

    x3 = _ffn_ln(x2, ffn2_w_in[i].astype(BF16), ffn2_w_out[i].astype(BF16), ln3_g[i][None], ln3_b[i][None])
    return x3.reshape(bsz, seq, D_MODEL)
```

```python
import functools
import math

import jax
import jax.numpy as jnp
from jax import lax
from jax.experimental import pallas as pl
from jax.experimental.pallas import tpu as pltpu

F32 = jnp.float32
BF16 = jnp.bfloat16

D_MODEL = 1024
D_FF = 2816
GM_WIDTH = 512
GM_GROUPS = 4
GM_GROUP_DIM = GM_WIDTH // GM_GROUPS
GM_CHUNK = 128
DIFF_HEADS = 4
DIFF_HEAD_DIM = 64
DIFF_V_DIM = 2 * DIFF_HEAD_DIM
DIFF_Q_WIDTH = DIFF_HEADS * 2 * DIFF_HEAD_DIM
DIFF_V_WIDTH = DIFF_HEADS * DIFF_V_DIM
MEM_HEADS = 4
MEM_HEAD_DIM = 64
MEM_WIDTH = MEM_HEADS * MEM_HEAD_DIM
N_BRANCH = 3
ROPE_THETA = 500000.0
ROPE_DIM = DIFF_HEAD_DIM // 4
ROPE_HALF = ROPE_DIM // 2
DEPTH = 1
DEEPNORM_ALPHA = (2 * DEPTH) ** 0.25
LN_EPS = 1e-5
LAM_INIT = 0.8 - 0.6 * math.exp(-0.3 * 0)
OFF_U = 0
OFF_V = OFF_U + GM_WIDTH
OFF_Q = OFF_V + GM_WIDTH
OFF_K = OFF_Q + DIFF_Q_WIDTH
OFF_VD = OFF_K + DIFF_Q_WIDTH
OFF_QM = OFF_VD + DIFF_V_WIDTH
OFF_GATE = OFF_QM + MEM_WIDTH
IN_WIDTH = OFF_GATE + N_BRANCH * D_MODEL

LANES = 128
BF16_SUBLANES = 16
VT_ROWS = DIFF_V_DIM + BF16_SUBLANES
MEM_VT_ROWS = MEM_HEAD_DIM + BF16_SUBLANES
MXU_COLS = 256
VMEM_LIMIT = 52 * 1024 * 1024

TM_FFN = 512
TM_PROJ = 512
TM_MERGE = 512
TQ = 512

NT_DIMS = (((1,), (1,)), ((), ()))


def _const_spec(shape):
    nd = len(shape)
    return pl.BlockSpec(shape, lambda *_: (0,) * nd, pipeline_mode=pl.Buffered(1))


def _params(n_axes):
    return pltpu.CompilerParams(dimension_semantics=("parallel",) * n_axes,
                                vmem_limit_bytes=VMEM_LIMIT)


def _params_sequential():
    return pltpu.CompilerParams(dimension_semantics=("arbitrary",), vmem_limit_bytes=VMEM_LIMIT)


def _layer_norm(z, g, b):
    mu = jnp.mean(z, axis=-1, keepdims=True)
    zc = z - mu
    var = jnp.mean(zc * zc, axis=-1, keepdims=True)
    return zc * lax.rsqrt(var + LN_EPS) * g + b


def _lagged_row_specs(n_tiles, tm):
    in_spec = lambda width: pl.BlockSpec((tm, width), lambda i: (jnp.minimum(i, n_tiles - 1), 0))
    out_spec = lambda width: pl.BlockSpec((tm, width), lambda i: (jnp.maximum(i - 1, 0), 0))
    return in_spec, out_spec


def _lagged_norm(z_ref, g_ref, b_ref, o_ref):
    n_groups = z_ref.shape[1] // MXU_COLS
    rows = z_ref.shape[0] // n_groups
    zero_rows = []
    for j in range(n_groups):
        rs = slice(j * rows, (j + 1) * rows)
        out = _layer_norm(z_ref[rs, :], g_ref[...], b_ref[...])
        o_ref[rs, :] = out
        bits = lax.bitcast_convert_type(jnp.max(out, axis=0, keepdims=True)[:, :MXU_COLS], jnp.uint32)
        zero_rows.append(lax.bitcast_convert_type((bits >> 16) >> 16, F32))
    return zero_rows


def _run_lagged(tile_fn, z_ref, g_ref, b_ref, o_ref):
    step = pl.program_id(0)
    last = pl.num_programs(0) - 1

    @pl.when(step == 0)
    def _():
        z_ref[...] = jnp.zeros(z_ref.shape, F32)

    @pl.when(step < last)
    def _():
        zero_rows = _lagged_norm(z_ref, g_ref, b_ref, o_ref)
        z_ref[...] = tile_fn(zero_rows)

    @pl.when(step == last)
    def _():
        _lagged_norm(z_ref, g_ref, b_ref, o_ref)


def _ffn_ln_body(x_ref, win_ref, wout_ref, g_ref, b_ref, o_ref, z_ref, h_ref):
    def tile(zero_rows):
        x = x_ref[...]
        xb = x.astype(BF16)
        hg = jnp.dot(xb, win_ref[:, :D_FF], preferred_element_type=F32)
        hu = jnp.dot(xb, win_ref[:, D_FF:], preferred_element_type=F32)
        for c in range(D_FF // MXU_COLS):
            cs = slice(c * MXU_COLS, (c + 1) * MXU_COLS)
            g = hg[:, cs] + zero_rows[c] if c < len(zero_rows) else hg[:, cs]
            h_ref[:, cs] = ((g + g * jnp.tanh(g)) * hu[:, cs]).astype(BF16)
        y = jnp.dot(h_ref[...], wout_ref[...], preferred_element_type=F32)
        return DEEPNORM_ALPHA * x + 0.5 * y

    _run_lagged(tile, z_ref, g_ref, b_ref, o_ref)


def _ffn_ln(x2d, w_in, w_out, g, b):
    t = x2d.shape[0]
    n_tiles = t // TM_FFN
    in_spec, out_spec = _lagged_row_specs(n_tiles, TM_FFN)
    return pl.pallas_call(
        _ffn_ln_body,
        grid=(n_tiles + 1,),
        in_specs=[
            in_spec(D_MODEL),
            _const_spec((D_MODEL, 2 * D_FF)),
            _const_spec((D_FF, D_MODEL)),
            _const_spec((1, D_MODEL)),
            _const_spec((1, D_MODEL)),
        ],
        out_specs=out_spec(D_MODEL),
        out_shape=jax.ShapeDtypeStruct((t, D_MODEL), F32),
        scratch_shapes=[pltpu.VMEM((TM_FFN, D_MODEL), F32), pltpu.VMEM((TM_FFN, D_FF), BF16)],
        compiler_params=_params_sequential(),
        name="ffn_ln",
    )(x2d, w_in, w_out, g, b)


def _mem_kv_body(m_ref, w_ref, km_ref, vmt_ref):
    n_mem = m_ref.shape[0]
    kv = jnp.dot(m_ref[...].astype(BF16), w_ref[...], preferred_element_type=F32)
    k = kv[:, :MEM_WIDTH].astype(BF16)
    vt = kv[:, MEM_WIDTH:].T.astype(BF16)
    lane = lax.broadcasted_iota(jnp.int32, k.shape, 1)
    ones = jnp.ones((MEM_VT_ROWS - MEM_HEAD_DIM, n_mem), BF16)
    for hh in range(MEM_HEADS):
        km_ref[0, hh] = jnp.where(lane // MEM_HEAD_DIM == hh, k, jnp.zeros_like(k))
        vmt_ref[0, hh * MEM_VT_ROWS:hh * MEM_VT_ROWS + MEM_HEAD_DIM] = vt[hh * MEM_HEAD_DIM:(hh + 1) * MEM_HEAD_DIM]
        vmt_ref[0, hh * MEM_VT_ROWS + MEM_HEAD_DIM:(hh + 1) * MEM_VT_ROWS] = ones


def _mem_kv(mem2d, w_kv, n_mem):
    bsz = mem2d.shape[0] // n_mem
    return pl.pallas_call(
        _mem_kv_body,
        grid=(bsz,),
        in_specs=[
            pl.BlockSpec((n_mem, D_MODEL), lambda i: (i, 0)),
            _const_spec((D_MODEL, 2 * MEM_WIDTH)),
        ],
        out_specs=[pl.BlockSpec((1, MEM_HEADS, n_mem, MEM_WIDTH), lambda i: (i, 0, 0, 0)),
                   pl.BlockSpec((1, MEM_HEADS * MEM_VT_ROWS, n_mem), lambda i: (i, 0, 0))],
        out_shape=[jax.ShapeDtypeStruct((bsz, MEM_HEADS, n_mem, MEM_WIDTH), BF16),
                   jax.ShapeDtypeStruct((bsz, MEM_HEADS * MEM_VT_ROWS, n_mem), BF16)],
        compiler_params=_params(1),
        name="mem_kv",
    )(mem2d, w_kv)


def _proj_body(x_ref, pos_ref, invf_ref, w_ref, gmg_ref, gmb_ref, ws_ref, bst_ref, km_ref, vmt_ref,
               q_ref, k_ref, vt_ref, ygm_ref, ymem_ref):
    tm = x_ref.shape[0]
    xb = x_ref[...].astype(BF16)
    h = jnp.dot(xb, w_ref[:, :OFF_GATE], preferred_element_type=F32)

    qm = (h[:, OFF_QM:OFF_QM + MEM_WIDTH] * (MEM_HEAD_DIM ** -0.5 * math.log2(math.e))).astype(BF16)
    st = [lax.dot_general(km_ref[0, hh], qm, NT_DIMS, preferred_element_type=F32) for hh in range(MEM_HEADS)]
    e = [jnp.exp2(s - jnp.max(s, axis=0, keepdims=True)).astype(BF16) for s in st]
    outs = []
    for hh in range(MEM_HEADS):
        ot = jnp.dot(vmt_ref[0, hh * MEM_VT_ROWS:(hh + 1) * MEM_VT_ROWS], e[hh], preferred_element_type=F32)
        outs.append(ot[:MEM_HEAD_DIM] * (1.0 / ot[MEM_HEAD_DIM:MEM_HEAD_DIM + 1]))
    ymem_ref[...] = jnp.concatenate(outs, axis=0).T.astype(BF16)

    u = jax.nn.gelu(h[:, OFF_U:OFF_U + GM_WIDTH])
    v = jax.nn.gelu(h[:, OFF_V:OFF_V + GM_WIDTH])
    vn = _layer_norm(v, gmg_ref[...], gmb_ref[...]).astype(BF16)
    row = lax.broadcasted_iota(jnp.int32, (GM_CHUNK, GM_CHUNK), 0)
    col = lax.broadcasted_iota(jnp.int32, (GM_CHUNK, GM_CHUNK), 1)
    causal = col <= row
    for g in range(GM_GROUPS):
        wc = jnp.where(causal, ws_ref[g], 0.0).astype(BF16)
        bcol = bst_ref[:, g:g + 1]
        gs = slice(g * GM_GROUP_DIM, (g + 1) * GM_GROUP_DIM)
        for c in range(tm // GM_CHUNK):
            rs = slice(c * GM_CHUNK, (c + 1) * GM_CHUNK)
            mixed = jnp.dot(wc, vn[rs, gs], preferred_element_type=F32) + bcol
            ygm_ref[rs, gs] = (u[rs, gs] * mixed).astype(BF16)

    ang = invf_ref[...] * pos_ref[0].astype(F32)
    cos_t = jnp.cos(ang)
    sin_t = jnp.sin(ang)
    rest = DIFF_HEAD_DIM - ROPE_DIM
    one_t = jnp.ones((rest, tm), F32)
    zero_t = jnp.zeros((rest, tm), F32)
    zero_h = jnp.zeros((ROPE_HALF, tm), F32)
    cos = jnp.concatenate([cos_t, cos_t, one_t] * 2, axis=0).T
    sin_lo = jnp.concatenate([-sin_t, zero_h, zero_t] * 2, axis=0).T
    sin_hi = jnp.concatenate([zero_h, sin_t, zero_t] * 2, axis=0).T

    def rope(t):
        return (t * cos + pltpu.roll(t, LANES - ROPE_HALF, axis=1) * sin_lo
                + pltpu.roll(t, ROPE_HALF, axis=1) * sin_hi)

    scale = DIFF_HEAD_DIM ** -0.5 * math.log2(math.e)
    for j in range(DIFF_Q_WIDTH // LANES):
        cs = slice(j * LANES, (j + 1) * LANES)
        q_ref[:, cs] = (rope(h[:, OFF_Q + j * LANES:OFF_Q + (j + 1) * LANES]) * scale).astype(BF16)
        k_ref[:, cs] = rope(h[:, OFF_K + j * LANES:OFF_K + (j + 1) * LANES]).astype(BF16)
    vt = h[:, OFF_VD:OFF_VD + DIFF_V_WIDTH].T.astype(BF16)
    ones = jnp.ones((VT_ROWS - DIFF_V_DIM, tm), BF16)
    for hh in range(DIFF_HEADS):
        vt_ref[0, hh * VT_ROWS:hh * VT_ROWS + DIFF_V_DIM] = vt[hh * DIFF_V_DIM:(hh + 1) * DIFF_V_DIM]
        vt_ref[0, hh * VT_ROWS + DIFF_V_DIM:(hh + 1) * VT_ROWS] = ones


def _proj(x2d, pos3d, invf, w, gm_g, gm_b, w_s, b_st, km, vmt, seq, n_mem):
    t = x2d.shape[0]
    tiles_per_batch = seq // TM_PROJ
    row_spec = lambda width: pl.BlockSpec((TM_PROJ, width), lambda i: (i, 0))
    return pl.pallas_call(
        _proj_body,
        grid=(t // TM_PROJ,),
        in_specs=[
            row_spec(D_MODEL),
            pl.BlockSpec((1, 1, TM_PROJ), lambda i: (i, 0, 0)),
            _const_spec((ROPE_HALF, 1)),
            _const_spec((D_MODEL, IN_WIDTH)),
            _const_spec((1, GM_WIDTH)),
            _const_spec((1, GM_WIDTH)),
            _const_spec((GM_GROUPS, GM_CHUNK, GM_CHUNK)),
            _const_spec((GM_CHUNK, GM_GROUPS)),
            pl.BlockSpec((1, MEM_HEADS, n_mem, MEM_WIDTH), lambda i: (i // tiles_per_batch, 0, 0, 0)),
            pl.BlockSpec((1, MEM_HEADS * MEM_VT_ROWS, n_mem), lambda i: (i // tiles_per_batch, 0, 0)),
        ],
        out_specs=[row_spec(DIFF_Q_WIDTH), row_spec(DIFF_Q_WIDTH),
                   pl.BlockSpec((1, DIFF_HEADS * VT_ROWS, TM_PROJ),
                                lambda i: (i // tiles_per_batch, 0, i % tiles_per_batch)),
                   row_spec(GM_WIDTH), row_spec(MEM_WIDTH)],
        out_shape=[jax.ShapeDtypeStruct((t, DIFF_Q_WIDTH), BF16),
                   jax.ShapeDtypeStruct((t, DIFF_Q_WIDTH), BF16),
                   jax.ShapeDtypeStruct((t // seq, DIFF_HEADS * VT_ROWS, seq), BF16),
                   jax.ShapeDtypeStruct((t, GM_WIDTH), BF16),
                   jax.ShapeDtypeStruct((t, MEM_WIDTH), BF16)],
        compiler_params=_params(1),
        name="proj",
    )(x2d, pos3d, invf, w, gm_g, gm_b, w_s, b_st, km, vmt)


def _diff_attn_body(lamv_ref, ngc_ref, q_ref, k_ref, vt_ref, o_ref, qs_ref, st_ref, e_ref, acc_ref):
    qi = pl.program_id(1)
    heads = range(DIFF_HEADS)
    lane = lax.broadcasted_iota(jnp.int32, (TQ, DIFF_V_DIM), 1)
    for h in heads:
        q = q_ref[0, :, h * DIFF_V_DIM:(h + 1) * DIFF_V_DIM]
        zero = jnp.zeros_like(q)
        qs_ref[h, :TQ] = jnp.where(lane < DIFF_HEAD_DIM, q, zero)
        qs_ref[h, TQ:] = jnp.where(lane >= DIFF_HEAD_DIM, q, zero)
    acc_ref[...] = jnp.zeros(acc_ref.shape, F32)
    e_ref[...] = jnp.zeros(e_ref.shape, BF16)

    def scores(h, ki):
        kb = k_ref[0, pl.ds(pl.multiple_of(ki * TQ, TQ), TQ), h * DIFF_V_DIM:(h + 1) * DIFF_V_DIM]
        st = lax.dot_general(kb, qs_ref[h], NT_DIMS, preferred_element_type=F32)
        st_ref[h] = st
        return jnp.max(st, axis=0, keepdims=True)

    def values(h, ki):
        vtb = vt_ref[0, h * VT_ROWS:(h + 1) * VT_ROWS, pl.ds(pl.multiple_of(ki * TQ, TQ), TQ)]
        return jnp.dot(vtb, e_ref[h], preferred_element_type=F32)

    def softmax(h, st, m, bm):
        m_new = jnp.maximum(m, bm)
        e_ref[h] = jnp.exp2(st - m_new).astype(BF16)
        return jnp.exp2(m - m_new), m_new

    def step(t, carry):
        a_prev, m, bm = carry
        pv = [values(h, jnp.maximum(t - 1, 0)) for h in heads]
        am = [softmax(h, st_ref[h], m[h], bm[h]) for h in heads]
        for h in heads:
            acc_ref[h] = a_prev[h] * acc_ref[h] + pv[h]
        return (tuple(x[0] for x in am), tuple(x[1] for x in am),
                tuple(scores(h, t + 1) for h in heads))

    init = (tuple(jnp.ones((1, 2 * TQ), F32) for _ in heads),
            tuple(jnp.full((1, 2 * TQ), -jnp.inf, F32) for _ in heads),
            tuple(scores(h, 0) for h in heads))
    a_prev, m, _ = lax.fori_loop(0, qi, step, init)

    lv = lamv_ref[...]
    lam = (jnp.exp(jnp.sum(lv[0:1] * lv[1:2], axis=-1, keepdims=True))
           - jnp.exp(jnp.sum(lv[2:3] * lv[3:4], axis=-1, keepdims=True)) + LAM_INIT)
    kpos = lax.broadcasted_iota(jnp.int32, (TQ, 2 * TQ), 0)
    qpos = lax.broadcasted_iota(jnp.int32, (TQ, 2 * TQ), 1) % TQ
    visible = kpos <= qpos
    pv = [values(h, jnp.maximum(qi - 1, 0)) for h in heads]
    for h in heads:
        st = jnp.where(visible, st_ref[h], -jnp.inf)
        a, _ = softmax(h, st, m[h], jnp.max(st, axis=0, keepdims=True))
        acc = a * (a_prev[h] * acc_ref[h] + pv[h]) + values(h, qi)
        on = acc[:DIFF_V_DIM] * (1.0 / acc[DIFF_V_DIM:DIFF_V_DIM + 1])
        ot = on[:, :TQ] - lam * on[:, TQ:]
        ot = ot * lax.rsqrt(jnp.mean(ot * ot, axis=0, keepdims=True) + LN_EPS) * ngc_ref[...]
        o_ref[0, :, h * DIFF_V_DIM:(h + 1) * DIFF_V_DIM] = (ot * (1.0 - LAM_INIT)).T.astype(BF16)


def _diff_attn(lamv, norm_g_col, q, k, vt):
    b, s, _ = q.shape
    return pl.pallas_call(
        _diff_attn_body,
        grid=(b, s // TQ),
        in_specs=[
            _const_spec((4, DIFF_HEAD_DIM)),
            _const_spec((DIFF_V_DIM, 1)),
            pl.BlockSpec((1, TQ, DIFF_Q_WIDTH), lambda bi, qi: (bi, qi, 0)),
            pl.BlockSpec((1, s, DIFF_Q_WIDTH), lambda bi, qi: (bi, 0, 0)),
            pl.BlockSpec((1, DIFF_HEADS * VT_ROWS, s), lambda bi, qi: (bi, 0, 0)),
        ],
        out_specs=pl.BlockSpec((1, TQ, DIFF_V_WIDTH), lambda bi, qi: (bi, qi, 0)),
        out_shape=jax.ShapeDtypeStruct((b, s, DIFF_V_WIDTH), BF16),
        scratch_shapes=[pltpu.VMEM((DIFF_HEADS, 2 * TQ, DIFF_V_DIM), BF16),
                        pltpu.VMEM((DIFF_HEADS, TQ, 2 * TQ), F32),
                        pltpu.VMEM((DIFF_HEADS, TQ, 2 * TQ), BF16),
                        pltpu.VMEM((DIFF_HEADS, VT_ROWS, 2 * TQ), F32)],
        compiler_params=_params(2),
        name="diff_attn",
    )(lamv, norm_g_col, q, k, vt)


def _merge_body(x_ref, ygm_ref, ydf_ref, ymm_ref, wgate_ref, gb_ref, wgm_ref, wdf_ref, wmm_ref, wo_ref,
                g_ref, b_ref, o_ref, z_ref, m_ref):
    def tile(zero_rows):
        x = x_ref[...]
        branches = (jnp.dot(ygm_ref[...], wgm_ref[...], preferred_element_type=F32),
                    jnp.dot(ydf_ref[...], wdf_ref[...], preferred_element_type=F32),
                    jnp.dot(ymm_ref[...], wmm_ref[...], preferred_element_type=F32))
        logits = jnp.dot(x.astype(BF16), wgate_ref[:, OFF_GATE:], preferred_element_type=F32)
        for c in range(D_MODEL // MXU_COLS):
            merged = zero_rows[c]
            for r in range(N_BRANCH):
                cs = slice(r * D_MODEL + c * MXU_COLS, r * D_MODEL + (c + 1) * MXU_COLS)
                y = branches[r][:, c * MXU_COLS:(c + 1) * MXU_COLS]
                merged = merged + (y + jnp.tanh(logits[:, cs] + gb_ref[:, cs]) * y)
            m_ref[:, c * MXU_COLS:(c + 1) * MXU_COLS] = merged.astype(BF16)
        y = jnp.dot(m_ref[...], wo_ref[...], preferred_element_type=F32)
        return DEEPNORM_ALPHA * x + y

    _run_lagged(tile, z_ref, g_ref, b_ref, o_ref)


def _merge(x2d, ygm, ydf, ymm, w_gate, gate_b, w_gm, w_df, w_mm, w_o, g, b):
    t = x2d.shape[0]
    n_tiles = t // TM_MERGE
    row_spec, out_spec = _lagged_row_specs(n_tiles, TM_MERGE)
    return pl.pallas_call(
        _merge_body,
        grid=(n_tiles + 1,),
        in_specs=[
            row_spec(D_MODEL), row_spec(GM_WIDTH), row_spec(DIFF_V_WIDTH), row_spec(MEM_WIDTH),
            _const_spec((D_MODEL, IN_WIDTH)),
            _const_spec((1, N_BRANCH * D_MODEL)),
            _const_spec((GM_WIDTH, D_MODEL)),
            _const_spec((DIFF_V_WIDTH, D_MODEL)),
            _const_spec((MEM_WIDTH, D_MODEL)),
            _const_spec((D_MODEL, D_MODEL)),
            _const_spec((1, D_MODEL)),
            _const_spec((1, D_MODEL)),
        ],
        out_specs=out_spec(D_MODEL),
        out_shape=jax.ShapeDtypeStruct((t, D_MODEL), F32),
        scratch_shapes=[pltpu.VMEM((TM_MERGE, D_MODEL), F32), pltpu.VMEM((TM_MERGE, D_MODEL), BF16)],
        compiler_params=_params_sequential(),
        name="merge",
    )(x2d, ygm, ydf, ymm, w_gate, gate_b, w_gm, w_df, w_mm, w_o, g, b)


def _rope_freqs():
    return (ROPE_THETA ** (-jnp.arange(0, ROPE_DIM, 2, dtype=F32) / ROPE_DIM)).reshape(ROPE_HALF, 1)


def kernel(x, mem, positions, ffn1_w_in, ffn1_w_out, ln1_g, ln1_b, w_in, gate_b, gm_ln_g, gm_ln_b, gm_w_s, gm_b_s,
           lambda_q1, lambda_k1, lambda_q2, lambda_k2, diff_norm_g, w_mem_kv, w_branch_gm, w_branch_diff,
           w_branch_mem, w_o, ln2_g, ln2_b, ffn2_w_in, ffn2_w_out, ln3_g, ln3_b):
    bsz, seq, _ = x.shape
    n_mem = mem.shape[1]
    t = bsz * seq
    i = 0
    x2d = x.reshape(t, D_MODEL)

    ffn_scale = jnp.concatenate([jnp.full((1, D_FF), 0.5, F32), jnp.ones((1, D_FF), F32)], axis=1)
    ffn_w_in = lambda w: (w * ffn_scale).astype(BF16)
    in_scale = jnp.concatenate([jnp.ones((1, OFF_GATE), F32), jnp.full((1, N_BRANCH * D_MODEL), 0.5, F32)], axis=1)
    w_in_b = (w_in[i] * in_scale).astype(BF16)

    x1 = _ffn_ln(x2d, ffn_w_in(ffn1_w_in[i]), ffn1_w_out[i].astype(BF16), ln1_g[i][None], ln1_b[i][None])

    km, vmt = _mem_kv(mem.reshape(bsz * n_mem, D_MODEL), w_mem_kv[i].astype(BF16), n_mem)
    q, k, vt, ygm, ymm = _proj(
        x1, positions.reshape(t // TM_PROJ, 1, TM_PROJ), _rope_freqs(), w_in_b,
        gm_ln_g[i][None], gm_ln_b[i][None], gm_w_s[i], gm_b_s[i].T, km, vmt, seq, n_mem)

    lamv = jnp.stack([lambda_q1[i], lambda_k1[i], lambda_q2[i], lambda_k2[i]]).astype(F32)
    ydf = _diff_attn(lamv, diff_norm_g[i][:, None],
                     q.reshape(bsz, seq, DIFF_Q_WIDTH), k.reshape(bsz, seq, DIFF_Q_WIDTH),
                     vt).reshape(t, DIFF_V_WIDTH)

    x2 = _merge(x1, ygm, ydf, ymm, w_in_b, 0.5 * gate_b[i][None],
                w_branch_gm[i].astype(BF16), w_branch_diff[i].astype(BF16), w_branch_mem[i].astype(BF16),
                (0.5 * w_o[i]).astype(BF16), ln2_g[i][None], ln2_b[i][None])

    x3 = _ffn_ln(x2, ffn_w_in(ffn2_w_in[i]), ffn2_w_out[i].astype(BF16), ln3_g[i][None], ln3_b[i][None])
    return x3.reshape(bsz, seq, D_MODEL)
```

```python
import functools
import math

import jax
import jax.numpy as jnp
from jax import lax
from jax.experimental import pallas as pl
from jax.experimental.pallas import tpu as pltpu

F32 = jnp.float32
BF16 = jnp.bfloat16

D_MODEL = 1024
D_FF = 2816
GM_WIDTH = 512
GM_GROUPS = 4
GM_GROUP_DIM = GM_WIDTH // GM_GROUPS
GM_CHUNK = 128
DIFF_HEADS = 4
DIFF_HEAD_DIM = 64
DIFF_V_DIM = 2 * DIFF_HEAD_DIM
DIFF_Q_WIDTH = DIFF_HEADS * 2 * DIFF_HEAD_DIM
DIFF_V_WIDTH = DIFF_HEADS * DIFF_V_DIM
MEM_HEADS = 4
MEM_HEAD_DIM = 64
MEM_WIDTH = MEM_HEADS * MEM_HEAD_DIM
N_BRANCH = 3
ROPE_THETA = 500000.0
ROPE_DIM = DIFF_HEAD_DIM // 4
ROPE_HALF = ROPE_DIM // 2
DEPTH = 1
DEEPNORM_ALPHA = (2 * DEPTH) ** 0.25
LN_EPS = 1e-5
LAM_INIT = 0.8 - 0.6 * math.exp(-0.3 * 0)
OFF_U = 0
OFF_V = OFF_U + GM_WIDTH
OFF_Q = OFF_V + GM_WIDTH
OFF_K = OFF_Q + DIFF_Q_WIDTH
OFF_VD = OFF_K + DIFF_Q_WIDTH
OFF_QM = OFF_VD + DIFF_V_WIDTH
OFF_GATE = OFF_QM + MEM_WIDTH
IN_WIDTH = OFF_GATE + N_BRANCH * D_MODEL

LANES = 128
BF16_SUBLANES = 16
VT_ROWS = DIFF_V_DIM + BF16_SUBLANES
MEM_VT_ROWS = MEM_HEAD_DIM + BF16_SUBLANES
MXU_COLS = 256
VMEM_LIMIT = 52 * 1024 * 1024

TM_FFN = 512
TM_PROJ = 512
TM_MERGE = 512
TQ = 512

NT_DIMS = (((1,), (1,)), ((), ()))


def _const_spec(shape):
    nd = len(shape)
    return pl.BlockSpec(shape, lambda *_: (0,) * nd, pipeline_mode=pl.Buffered(1))


def _params(n_axes):
    return pltpu.CompilerParams(dimension_semantics=("parallel",) * n_axes,
                                vmem_limit_bytes=VMEM_LIMIT)


def _params_sequential():
    return pltpu.CompilerParams(dimension_semantics=("arbitrary",), vmem_limit_bytes=VMEM_LIMIT)


def _layer_norm(z, g, b):
    mu = jnp.mean(z, axis=-1, keepdims=True)
    zc = z - mu
    var = jnp.mean(zc * zc, axis=-1, keepdims=True)
    return zc * lax.rsqrt(var + LN_EPS) * g + b


def _lagged_row_specs(n_tiles, tm):
    in_spec = lambda width: pl.BlockSpec((tm, width), lambda i: (jnp.minimum(i, n_tiles - 1), 0))
    out_spec = lambda width: pl.BlockSpec((tm, width), lambda i: (jnp.maximum(i - 1, 0), 0))
    return in_spec, out_spec


def _lagged_norm(z_ref, g_ref, b_ref, o_ref):
    n_groups = z_ref.shape[1] // MXU_COLS
    rows = z_ref.shape[0] // n_groups
    zero_rows = []
    for j in range(n_groups):
        rs = slice(j * rows, (j + 1) * rows)
        out = _layer_norm(z_ref[rs, :], g_ref[...], b_ref[...])
        o_ref[rs, :] = out
        bits = lax.bitcast_convert_type(jnp.max(out, axis=0, keepdims=True)[:, :MXU_COLS], jnp.uint32)
        zero_rows.append(lax.bitcast_convert_type((bits >> 16) >> 16, F32))
    return zero_rows


def _run_lagged(tile_fn, z_ref, g_ref, b_ref, o_ref):
    step = pl.program_id(0)
    last = pl.num_programs(0) - 1

    @pl.when(step == 0)
    def _():
        z_ref[...] = jnp.zeros(z_ref.shape, F32)

    @pl.when(step < last)
    def _():
        zero_rows = _lagged_norm(z_ref, g_ref, b_ref, o_ref)
        z_ref[...] = tile_fn(zero_rows)

    @pl.when(step == last)
    def _():
        _lagged_norm(z_ref, g_ref, b_ref, o_ref)


def _ffn_ln_body(x_ref, win_ref, wout_ref, g_ref, b_ref, o_ref, z_ref, h_ref):
    def tile(zero_rows):
        x = x_ref[...]
        xb = x.astype(BF16)
        hg = jnp.dot(xb, win_ref[:, :D_FF], preferred_element_type=F32)
        hu = jnp.dot(xb, win_ref[:, D_FF:], preferred_element_type=F32)
        for c in range(D_FF // MXU_COLS):
            cs = slice(c * MXU_COLS, (c + 1) * MXU_COLS)
            g = hg[:, cs] + zero_rows[c] if c < len(zero_rows) else hg[:, cs]
            h_ref[:, cs] = ((g + g * jnp.tanh(g)) * hu[:, cs]).astype(BF16)
        y = jnp.dot(h_ref[...], wout_ref[...], preferred_element_type=F32)
        return DEEPNORM_ALPHA * x + 0.5 * y

    _run_lagged(tile, z_ref, g_ref, b_ref, o_ref)


def _ffn_ln(x2d, w_in, w_out, g, b):
    t = x2d.shape[0]
    n_tiles = t // TM_FFN
    in_spec, out_spec = _lagged_row_specs(n_tiles, TM_FFN)
    return pl.pallas_call(
        _ffn_ln_body,
        grid=(n_tiles + 1,),
        in_specs=[
            in_spec(D_MODEL),
            _const_spec((D_MODEL, 2 * D_FF)),
            _const_spec((D_FF, D_MODEL)),
            _const_spec((1, D_MODEL)),
            _const_spec((1, D_MODEL)),
        ],
        out_specs=out_spec(D_MODEL),
        out_shape=jax.ShapeDtypeStruct((t, D_MODEL), F32),
        scratch_shapes=[pltpu.VMEM((TM_FFN, D_MODEL), F32), pltpu.VMEM((TM_FFN, D_FF), BF16)],
        compiler_params=_params_sequential(),
        name="ffn_ln",
    )(x2d, w_in, w_out, g, b)


def _mem_kv_body(m_ref, w_ref, km_ref, vmt_ref):
    n_mem = m_ref.shape[0]
    kv = jnp.dot(m_ref[...].astype(BF16), w_ref[...], preferred_element_type=F32)
    k = kv[:, :MEM_WIDTH].astype(BF16)
    vt = kv[:, MEM_WIDTH:].T.astype(BF16)
    lane = lax.broadcasted_iota(jnp.int32, k.shape, 1)
    ones = jnp.ones((MEM_VT_ROWS - MEM_HEAD_DIM, n_mem), BF16)
    for hh in range(MEM_HEADS):
        km_ref[0, hh] = jnp.where(lane // MEM_HEAD_DIM == hh, k, jnp.zeros_like(k))
        vmt_ref[0, hh * MEM_VT_ROWS:hh * MEM_VT_ROWS + MEM_HEAD_DIM] = vt[hh * MEM_HEAD_DIM:(hh + 1) * MEM_HEAD_DIM]
        vmt_ref[0, hh * MEM_VT_ROWS + MEM_HEAD_DIM:(hh + 1) * MEM_VT_ROWS] = ones


def _mem_kv(mem2d, w_kv, n_mem):
    bsz = mem2d.shape[0] // n_mem
    return pl.pallas_call(
        _mem_kv_body,
        grid=(bsz,),
        in_specs=[
            pl.BlockSpec((n_mem, D_MODEL), lambda i: (i, 0)),
            _const_spec((D_MODEL, 2 * MEM_WIDTH)),
        ],
        out_specs=[pl.BlockSpec((1, MEM_HEADS, n_mem, MEM_WIDTH), lambda i: (i, 0, 0, 0)),
                   pl.BlockSpec((1, MEM_HEADS * MEM_VT_ROWS, n_mem), lambda i: (i, 0, 0))],
        out_shape=[jax.ShapeDtypeStruct((bsz, MEM_HEADS, n_mem, MEM_WIDTH), BF16),
                   jax.ShapeDtypeStruct((bsz, MEM_HEADS * MEM_VT_ROWS, n_mem), BF16)],
        compiler_params=_params(1),
        name="mem_kv",
    )(mem2d, w_kv)


def _proj_body(x_ref, pos_ref, invf_ref, w_ref, gmg_ref, gmb_ref, ws_ref, bst_ref, km_ref, vmt_ref,
               q_ref, k_ref, vt_ref, ygm_ref, ymem_ref):
    tm = x_ref.shape[0]
    xb = x_ref[...].astype(BF16)

    def project(off, width):
        return jnp.dot(xb, w_ref[:, off:off + width], preferred_element_type=F32)

    h_qm = project(OFF_QM, MEM_WIDTH)
    h_uv = project(OFF_U, 2 * GM_WIDTH)

    qm = (h_qm * (MEM_HEAD_DIM ** -0.5 * math.log2(math.e))).astype(BF16)
    st = [lax.dot_general(km_ref[0, hh], qm, NT_DIMS, preferred_element_type=F32) for hh in range(MEM_HEADS)]
    h_qk = project(OFF_Q, 2 * DIFF_Q_WIDTH)
    e = [jnp.exp2(s - jnp.max(s, axis=0, keepdims=True)).astype(BF16) for s in st]
    outs = []
    for hh in range(MEM_HEADS):
        ot = jnp.dot(vmt_ref[0, hh * MEM_VT_ROWS:(hh + 1) * MEM_VT_ROWS], e[hh], preferred_element_type=F32)
        outs.append(ot[:MEM_HEAD_DIM] * (1.0 / ot[MEM_HEAD_DIM:MEM_HEAD_DIM + 1]))
    ymem_ref[...] = jnp.concatenate(outs, axis=0).T.astype(BF16)

    u = jax.nn.gelu(h_uv[:, :GM_WIDTH])
    v = jax.nn.gelu(h_uv[:, GM_WIDTH:])
    vn = _layer_norm(v, gmg_ref[...], gmb_ref[...]).astype(BF16)
    row = lax.broadcasted_iota(jnp.int32, (GM_CHUNK, GM_CHUNK), 0)
    col = lax.broadcasted_iota(jnp.int32, (GM_CHUNK, GM_CHUNK), 1)
    causal = col <= row
    for g in range(GM_GROUPS):
        wc = jnp.where(causal, ws_ref[g], 0.0).astype(BF16)
        bcol = bst_ref[:, g:g + 1]
        gs = slice(g * GM_GROUP_DIM, (g + 1) * GM_GROUP_DIM)
        for c in range(tm // GM_CHUNK):
            rs = slice(c * GM_CHUNK, (c + 1) * GM_CHUNK)
            mixed = jnp.dot(wc, vn[rs, gs], preferred_element_type=F32) + bcol
            ygm_ref[rs, gs] = (u[rs, gs] * mixed).astype(BF16)
    h_vd = project(OFF_VD, DIFF_V_WIDTH)

    ang = invf_ref[...] * pos_ref[0].astype(F32)
    cos_t = jnp.cos(ang)
    sin_t = jnp.sin(ang)
    rest = DIFF_HEAD_DIM - ROPE_DIM
    one_t = jnp.ones((rest, tm), F32)
    zero_t = jnp.zeros((rest, tm), F32)
    zero_h = jnp.zeros((ROPE_HALF, tm), F32)
    cos = jnp.concatenate([cos_t, cos_t, one_t] * 2, axis=0).T
    sin_lo = jnp.concatenate([-sin_t, zero_h, zero_t] * 2, axis=0).T
    sin_hi = jnp.concatenate([zero_h, sin_t, zero_t] * 2, axis=0).T

    def rope(t):
        return (t * cos + pltpu.roll(t, LANES - ROPE_HALF, axis=1) * sin_lo
                + pltpu.roll(t, ROPE_HALF, axis=1) * sin_hi)

    scale = DIFF_HEAD_DIM ** -0.5 * math.log2(math.e)
    for j in range(DIFF_Q_WIDTH // LANES):
        cs = slice(j * LANES, (j + 1) * LANES)
        q_ref[:, cs] = (rope(h_qk[:, j * LANES:(j + 1) * LANES]) * scale).astype(BF16)
        k_ref[:, cs] = rope(h_qk[:, DIFF_Q_WIDTH + j * LANES:DIFF_Q_WIDTH + (j + 1) * LANES]).astype(BF16)
    vt = h_vd.T.astype(BF16)
    ones = jnp.ones((VT_ROWS - DIFF_V_DIM, tm), BF16)
    for hh in range(DIFF_HEADS):
        vt_ref[0, hh * VT_ROWS:hh * VT_ROWS + DIFF_V_DIM] = vt[hh * DIFF_V_DIM:(hh + 1) * DIFF_V_DIM]
        vt_ref[0, hh * VT_ROWS + DIFF_V_DIM:(hh + 1) * VT_ROWS] = ones


def _proj(x2d, pos3d, invf, w, gm_g, gm_b, w_s, b_st, km, vmt, seq, n_mem):
    t = x2d.shape[0]
    tiles_per_batch = seq // TM_PROJ
    row_spec = lambda width: pl.BlockSpec((TM_PROJ, width), lambda i: (i, 0))
    return pl.pallas_call(
        _proj_body,
        grid=(t // TM_PROJ,),
        in_specs=[
            row_spec(D_MODEL),
            pl.BlockSpec((1, 1, TM_PROJ), lambda i: (i, 0, 0)),
            _const_spec((ROPE_HALF, 1)),
            _const_spec((D_MODEL, IN_WIDTH)),
            _const_spec((1, GM_WIDTH)),
            _const_spec((1, GM_WIDTH)),
            _const_spec((GM_GROUPS, GM_CHUNK, GM_CHUNK)),
            _const_spec((GM_CHUNK, GM_GROUPS)),
            pl.BlockSpec((1, MEM_HEADS, n_mem, MEM_WIDTH), lambda i: (i // tiles_per_batch, 0, 0, 0)),
            pl.BlockSpec((1, MEM_HEADS * MEM_VT_ROWS, n_mem), lambda i: (i // tiles_per_batch, 0, 0)),
        ],
        out_specs=[row_spec(DIFF_Q_WIDTH), row_spec(DIFF_Q_WIDTH),
                   pl.BlockSpec((1, DIFF_HEADS * VT_ROWS, TM_PROJ),
                                lambda i: (i // tiles_per_batch, 0, i % tiles_per_batch)),
                   row_spec(GM_WIDTH), row_spec(MEM_WIDTH)],
        out_shape=[jax.ShapeDtypeStruct((t, DIFF_Q_WIDTH), BF16),
                   jax.ShapeDtypeStruct((t, DIFF_Q_WIDTH), BF16),
                   jax.ShapeDtypeStruct((t // seq, DIFF_HEADS * VT_ROWS, seq), BF16),
                   jax.ShapeDtypeStruct((t, GM_WIDTH), BF16),
                   jax.ShapeDtypeStruct((t, MEM_WIDTH), BF16)],
        compiler_params=_params(1),
        name="proj",
    )(x2d, pos3d, invf, w, gm_g, gm_b, w_s, b_st, km, vmt)


def _diff_attn_body(lamv_ref, ngc_ref, q_ref, k_ref, vt_ref, o_ref, qs_ref, st_ref, e_ref, acc_ref):
    qi = pl.program_id(1)
    heads = range(DIFF_HEADS)
    lane = lax.broadcasted_iota(jnp.int32, (TQ, DIFF_V_DIM), 1)
    for h in heads:
        q = q_ref[0, :, h * DIFF_V_DIM:(h + 1) * DIFF_V_DIM]
        zero = jnp.zeros_like(q)
        qs_ref[h, :TQ] = jnp.where(lane < DIFF_HEAD_DIM, q, zero)
        qs_ref[h, TQ:] = jnp.where(lane >= DIFF_HEAD_DIM, q, zero)
    acc_ref[...] = jnp.zeros(acc_ref.shape, F32)
    e_ref[...] = jnp.zeros(e_ref.shape, BF16)

    def scores(h, ki):
        kb = k_ref[0, pl.ds(pl.multiple_of(ki * TQ, TQ), TQ), h * DIFF_V_DIM:(h + 1) * DIFF_V_DIM]
        st = lax.dot_general(kb, qs_ref[h], NT_DIMS, preferred_element_type=F32)
        st_ref[h] = st
        return jnp.max(st, axis=0, keepdims=True)

    def values(h, ki):
        vtb = vt_ref[0, h * VT_ROWS:(h + 1) * VT_ROWS, pl.ds(pl.multiple_of(ki * TQ, TQ), TQ)]
        return jnp.dot(vtb, e_ref[h], preferred_element_type=F32)

    def softmax(h, st, m, bm):
        m_new = jnp.maximum(m, bm)
        e_ref[h] = jnp.exp2(st - m_new).astype(BF16)
        return jnp.exp2(m - m_new), m_new

    def step(t, carry):
        a_prev, m, bm = carry
        pv = [values(h, jnp.maximum(t - 1, 0)) for h in heads]
        am = [softmax(h, st_ref[h], m[h], bm[h]) for h in heads]
        for h in heads:
            acc_ref[h] = a_prev[h] * acc_ref[h] + pv[h]
        return (tuple(x[0] for x in am), tuple(x[1] for x in am),
                tuple(scores(h, t + 1) for h in heads))

    init = (tuple(jnp.ones((1, 2 * TQ), F32) for _ in heads),
            tuple(jnp.full((1, 2 * TQ), -jnp.inf, F32) for _ in heads),
            tuple(scores(h, 0) for h in heads))
    a_prev, m, _ = lax.fori_loop(0, qi, step, init)

    lv = lamv_ref[...]
    lam = (jnp.exp(jnp.sum(lv[0:1] * lv[1:2], axis=-1, keepdims=True))
           - jnp.exp(jnp.sum(lv[2:3] * lv[3:4], axis=-1, keepdims=True)) + LAM_INIT)
    kpos = lax.broadcasted_iota(jnp.int32, (TQ, 2 * TQ), 0)
    qpos = lax.broadcasted_iota(jnp.int32, (TQ, 2 * TQ), 1) % TQ
    visible = kpos <= qpos
    pv = [values(h, jnp.maximum(qi - 1, 0)) for h in heads]
    for h in heads:
        st = jnp.where(visible, st_ref[h], -jnp.inf)
        a, _ = softmax(h, st, m[h], jnp.max(st, axis=0, keepdims=True))
        acc = a * (a_prev[h] * acc_ref[h] + pv[h]) + values(h, qi)
        on = acc[:DIFF_V_DIM] * (1.0 / acc[DIFF_V_DIM:DIFF_V_DIM + 1])
        ot = on[:, :TQ] - lam * on[:, TQ:]
        ot = ot * lax.rsqrt(jnp.mean(ot * ot, axis=0, keepdims=True) + LN_EPS) * ngc_ref[...]
        o_ref[0, :, h * DIFF_V_DIM:(h + 1) * DIFF_V_DIM] = (ot * (1.0 - LAM_INIT)).T.astype(BF16)


def _diff_attn(lamv, norm_g_col, q, k, vt):
    b, s, _ = q.shape
    return pl.pallas_call(
        _diff_attn_body,
        grid=(b, s // TQ),
        in_specs=[
            _const_spec((4, DIFF_HEAD_DIM)),
            _const_spec((DIFF_V_DIM, 1)),
            pl.BlockSpec((1, TQ, DIFF_Q_WIDTH), lambda bi, qi: (bi, qi, 0)),
            pl.BlockSpec((1, s, DIFF_Q_WIDTH), lambda bi, qi: (bi, 0, 0)),
            pl.BlockSpec((1, DIFF_HEADS * VT_ROWS, s), lambda bi, qi: (bi, 0, 0)),
        ],
        out_specs=pl.BlockSpec((1, TQ, DIFF_V_WIDTH), lambda bi, qi: (bi, qi, 0)),
        out_shape=jax.ShapeDtypeStruct((b, s, DIFF_V_WIDTH), BF16),
        scratch_shapes=[pltpu.VMEM((DIFF_HEADS, 2 * TQ, DIFF_V_DIM), BF16),
                        pltpu.VMEM((DIFF_HEADS, TQ, 2 * TQ), F32),
                        pltpu.VMEM((DIFF_HEADS, TQ, 2 * TQ), BF16),
                        pltpu.VMEM((DIFF_HEADS, VT_ROWS, 2 * TQ), F32)],
        compiler_params=_params(2),
        name="diff_attn",
    )(lamv, norm_g_col, q, k, vt)


def _merge_body(x_ref, ygm_ref, ydf_ref, ymm_ref, wgate_ref, gb_ref, wgm_ref, wdf_ref, wmm_ref, wo_ref,
                g_ref, b_ref, o_ref, z_ref, m_ref):
    def tile(zero_rows):
        x = x_ref[...]
        branches = (jnp.dot(ygm_ref[...], wgm_ref[...], preferred_element_type=F32),
                    jnp.dot(ydf_ref[...], wdf_ref[...], preferred_element_type=F32),
                    jnp.dot(ymm_ref[...], wmm_ref[...], preferred_element_type=F32))
        logits = jnp.dot(x.astype(BF16), wgate_ref[:, OFF_GATE:], preferred_element_type=F32)
        for c in range(D_MODEL // MXU_COLS):
            merged = zero_rows[c]
            for r in range(N_BRANCH):
                cs = slice(r * D_MODEL + c * MXU_COLS, r * D_MODEL + (c + 1) * MXU_COLS)
                y = branches[r][:, c * MXU_COLS:(c + 1) * MXU_COLS]
                merged = merged + (y + jnp.tanh(logits[:, cs] + gb_ref[:, cs]) * y)
            m_ref[:, c * MXU_COLS:(c + 1) * MXU_COLS] = merged.astype(BF16)
        y = jnp.dot(m_ref[...], wo_ref[...], preferred_element_type=F32)
        return DEEPNORM_ALPHA * x + y

    _run_lagged(tile, z_ref, g_ref, b_ref, o_ref)


def _merge(x2d, ygm, ydf, ymm, w_gate, gate_b, w_gm, w_df, w_mm, w_o, g, b):
    t = x2d.shape[0]
    n_tiles = t // TM_MERGE
    row_spec, out_spec = _lagged_row_specs(n_tiles, TM_MERGE)
    return pl.pallas_call(
        _merge_body,
        grid=(n_tiles + 1,),
        in_specs=[
            row_spec(D_MODEL), row_spec(GM_WIDTH), row_spec(DIFF_V_WIDTH), row_spec(MEM_WIDTH),
            _const_spec((D_MODEL, IN_WIDTH)),
            _const_spec((1, N_BRANCH * D_MODEL)),
            _const_spec((GM_WIDTH, D_MODEL)),
            _const_spec((DIFF_V_WIDTH, D_MODEL)),
            _const_spec((MEM_WIDTH, D_MODEL)),
            _const_spec((D_MODEL, D_MODEL)),
            _const_spec((1, D_MODEL)),
            _const_spec((1, D_MODEL)),
        ],
        out_specs=out_spec(D_MODEL),
        out_shape=jax.ShapeDtypeStruct((t, D_MODEL), F32),
        scratch_shapes=[pltpu.VMEM((TM_MERGE, D_MODEL), F32), pltpu.VMEM((TM_MERGE, D_MODEL), BF16)],
        compiler_params=_params_sequential(),
        name="merge",
    )(x2d, ygm, ydf, ymm, w_gate, gate_b, w_gm, w_df, w_mm, w_o, g, b)


def _rope_freqs():
    return (ROPE_THETA ** (-jnp.arange(0, ROPE_DIM, 2, dtype=F32) / ROPE_DIM)).reshape(ROPE_HALF, 1)


def kernel(x, mem, positions, ffn1_w_in, ffn1_w_out, ln1_g, ln1_b, w_in, gate_b, gm_ln_g, gm_ln_b, gm_w_s, gm_b_s,
           lambda_q1, lambda_k1, lambda_q2, lambda_k2, diff_norm_g, w_mem_kv, w_branch_gm, w_branch_diff,
           w_branch_mem, w_o, ln2_g, ln2_b, ffn2_w_in, ffn2_w_out, ln3_g, ln3_b):
    bsz, seq, _ = x.shape
    n_mem = mem.shape[1]
    t = bsz * seq
    i = 0
    x2d = x.reshape(t, D_MODEL)

    ffn_scale = jnp.concatenate([jnp.full((1, D_FF), 0.5, F32), jnp.ones((1, D_FF), F32)], axis=1)
    ffn_w_in = lambda w: (w * ffn_scale).astype(BF16)
    in_scale = jnp.concatenate([jnp.ones((1, OFF_GATE), F32), jnp.full((1, N_BRANCH * D_MODEL), 0.5, F32)], axis=1)
    w_in_b = (w_in[i] * in_scale).astype(BF16)

    x1 = _ffn_ln(x2d, ffn_w_in(ffn1_w_in[i]), ffn1_w_out[i].astype(BF16), ln1_g[i][None], ln1_b[i][None])

    km, vmt = _mem_kv(mem.reshape(bsz * n_mem, D_MODEL), w_mem_kv[i].astype(BF16), n_mem)
    q, k, vt, ygm, ymm = _proj(
        x1, positions.reshape(t // TM_PROJ, 1, TM_PROJ), _rope_freqs(), w_in_b,
        gm_ln_g[i][None], gm_ln_b[i][None], gm_w_s[i], gm_b_s[i].T, km, vmt, seq, n_mem)

    lamv = jnp.stack([lambda_q1[i], lambda_k1[i], lambda_q2[i], lambda_k2[i]]).astype(F32)
    ydf = _diff_attn(lamv, diff_norm_g[i][:, None],
                     q.reshape(bsz, seq, DIFF_Q_WIDTH), k.reshape(bsz, seq, DIFF_Q_WIDTH),
                     vt).reshape(t, DIFF_V_WIDTH)

    x2 = _merge(x1, ygm, ydf, ymm, w_in_b, 0.5 * gate_b[i][None],
                w_branch_gm[i].astype(BF16), w_branch_diff[i].astype(BF16), w_branch_mem[i].astype(BF16),
                (0.5 * w_o[i]).astype(BF16), ln2_g[i][None], ln2_b[i][None])

    x3 = _ffn_ln(x2, ffn_w_in(ffn2_w_in[i]), ffn2_w_out[i].astype(BF16), ln3_g[i][None], ln3_b[i][None])
    return x3.reshape(bsz, seq, D_MODEL)
```

```python
import functools
import math

import jax
import jax.numpy as jnp
from jax import lax
from jax.experimental import pallas as pl
from jax.experimental.pallas import tpu as pltpu

F32 = jnp.float32
BF16 = jnp.bfloat16

D_MODEL = 1024
D_FF = 2816
GM_WIDTH = 512
GM_GROUPS = 4
GM_GROUP_DIM = GM_WIDTH // GM_GROUPS
GM_CHUNK = 128
DIFF_HEADS = 4
DIFF_HEAD_DIM = 64
DIFF_V_DIM = 2 * DIFF_HEAD_DIM
DIFF_Q_WIDTH = DIFF_HEADS * 2 * DIFF_HEAD_DIM
DIFF_V_WIDTH = DIFF_HEADS * DIFF_V_DIM
MEM_HEADS = 4
MEM_HEAD_DIM = 64
MEM_WIDTH = MEM_HEADS * MEM_HEAD_DIM
N_BRANCH = 3
ROPE_THETA = 500000.0
ROPE_DIM = DIFF_HEAD_DIM // 4
ROPE_HALF = ROPE_DIM // 2
DEPTH = 1
DEEPNORM_ALPHA = (2 * DEPTH) ** 0.25
LN_EPS = 1e-5
LAM_INIT = 0.8 - 0.6 * math.exp(-0.3 * 0)
OFF_U = 0
OFF_V = OFF_U + GM_WIDTH
OFF_Q = OFF_V + GM_WIDTH
OFF_K = OFF_Q + DIFF_Q_WIDTH
OFF_VD = OFF_K + DIFF_Q_WIDTH
OFF_QM = OFF_VD + DIFF_V_WIDTH
OFF_GATE = OFF_QM + MEM_WIDTH
IN_WIDTH = OFF_GATE + N_BRANCH * D_MODEL

LANES = 128
BF16_SUBLANES = 16
VT_ROWS = DIFF_V_DIM + BF16_SUBLANES
MEM_VT_ROWS = MEM_HEAD_DIM + BF16_SUBLANES
MXU_COLS = 256
VMEM_LIMIT = 52 * 1024 * 1024

TM_FFN = 1024
TM_PROJ = 512
TM_MERGE = 1024
TQ = 512

NT_DIMS = (((1,), (1,)), ((), ()))


def _const_spec(shape):
    nd = len(shape)
    return pl.BlockSpec(shape, lambda *_: (0,) * nd, pipeline_mode=pl.Buffered(1))


def _params(n_axes):
    return pltpu.CompilerParams(dimension_semantics=("parallel",) * n_axes,
                                vmem_limit_bytes=VMEM_LIMIT)


def _params_sequential():
    return pltpu.CompilerParams(dimension_semantics=("arbitrary",), vmem_limit_bytes=VMEM_LIMIT)


def _layer_norm(z, g, b):
    mu = jnp.mean(z, axis=-1, keepdims=True)
    zc = z - mu
    var = jnp.mean(zc * zc, axis=-1, keepdims=True)
    return zc * lax.rsqrt(var + LN_EPS) * g + b


def _lagged_row_specs(n_tiles, tm):
    in_spec = lambda width: pl.BlockSpec((tm, width), lambda i: (jnp.minimum(i, n_tiles - 1), 0))
    out_spec = lambda width: pl.BlockSpec((tm, width), lambda i: (jnp.maximum(i - 1, 0), 0))
    return in_spec, out_spec


def _lagged_norm(z_ref, g_ref, b_ref, o_ref):
    n_groups = z_ref.shape[1] // MXU_COLS
    rows = z_ref.shape[0] // n_groups
    zero_rows = []
    for j in range(n_groups):
        rs = slice(j * rows, (j + 1) * rows)
        out = _layer_norm(z_ref[rs, :], g_ref[...], b_ref[...])
        o_ref[rs, :] = out
        bits = lax.bitcast_convert_type(jnp.max(out, axis=0, keepdims=True)[:, :MXU_COLS], jnp.uint32)
        zero_rows.append(lax.bitcast_convert_type((bits >> 16) >> 16, F32))
    return zero_rows


def _run_lagged(tile_fn, z_ref, g_ref, b_ref, o_ref):
    step = pl.program_id(0)
    last = pl.num_programs(0) - 1

    @pl.when(step == 0)
    def _():
        z_ref[...] = jnp.zeros(z_ref.shape, F32)

    @pl.when(step < last)
    def _():
        zero_rows = _lagged_norm(z_ref, g_ref, b_ref, o_ref)
        z_ref[...] = tile_fn(zero_rows)

    @pl.when(step == last)
    def _():
        _lagged_norm(z_ref, g_ref, b_ref, o_ref)


def _ffn_ln_body(x_ref, win_ref, wout_ref, g_ref, b_ref, o_ref, z_ref, h_ref):
    def tile(zero_rows):
        x = x_ref[...]
        xb = x.astype(BF16)
        for c in range(D_FF // MXU_COLS):
            cs = slice(c * MXU_COLS, (c + 1) * MXU_COLS)
            g = jnp.dot(xb, win_ref[:, cs], preferred_element_type=F32)
            u = jnp.dot(xb, win_ref[:, D_FF + c * MXU_COLS:D_FF + (c + 1) * MXU_COLS], preferred_element_type=F32)
            if c < len(zero_rows):
                g = g + zero_rows[c]
            h_ref[:, cs] = ((g + g * jnp.tanh(g)) * u).astype(BF16)
        y = jnp.dot(h_ref[...], wout_ref[...], preferred_element_type=F32)
        return DEEPNORM_ALPHA * x + 0.5 * y

    _run_lagged(tile, z_ref, g_ref, b_ref, o_ref)


def _ffn_ln(x2d, w_in, w_out, g, b):
    t = x2d.shape[0]
    n_tiles = t // TM_FFN
    in_spec, out_spec = _lagged_row_specs(n_tiles, TM_FFN)
    return pl.pallas_call(
        _ffn_ln_body,
        grid=(n_tiles + 1,),
        in_specs=[
            in_spec(D_MODEL),
            _const_spec((D_MODEL, 2 * D_FF)),
            _const_spec((D_FF, D_MODEL)),
            _const_spec((1, D_MODEL)),
            _const_spec((1, D_MODEL)),
        ],
        out_specs=out_spec(D_MODEL),
        out_shape=jax.ShapeDtypeStruct((t, D_MODEL), F32),
        scratch_shapes=[pltpu.VMEM((TM_FFN, D_MODEL), F32), pltpu.VMEM((TM_FFN, D_FF), BF16)],
        compiler_params=_params_sequential(),
        name="ffn_ln",
    )(x2d, w_in, w_out, g, b)


def _mem_kv_body(m_ref, w_ref, km_ref, vmt_ref):
    n_mem = m_ref.shape[0]
    kv = jnp.dot(m_ref[...].astype(BF16), w_ref[...], preferred_element_type=F32)
    k = kv[:, :MEM_WIDTH].astype(BF16)
    vt = kv[:, MEM_WIDTH:].T.astype(BF16)
    lane = lax.broadcasted_iota(jnp.int32, k.shape, 1)
    ones = jnp.ones((MEM_VT_ROWS - MEM_HEAD_DIM, n_mem), BF16)
    for hh in range(MEM_HEADS):
        km_ref[0, hh] = jnp.where(lane // MEM_HEAD_DIM == hh, k, jnp.zeros_like(k))
        vmt_ref[0, hh * MEM_VT_ROWS:hh * MEM_VT_ROWS + MEM_HEAD_DIM] = vt[hh * MEM_HEAD_DIM:(hh + 1) * MEM_HEAD_DIM]
        vmt_ref[0, hh * MEM_VT_ROWS + MEM_HEAD_DIM:(hh + 1) * MEM_VT_ROWS] = ones


def _mem_kv(mem2d, w_kv, n_mem):
    bsz = mem2d.shape[0] // n_mem
    return pl.pallas_call(
        _mem_kv_body,
        grid=(bsz,),
        in_specs=[
            pl.BlockSpec((n_mem, D_MODEL), lambda i: (i, 0)),
            _const_spec((D_MODEL, 2 * MEM_WIDTH)),
        ],
        out_specs=[pl.BlockSpec((1, MEM_HEADS, n_mem, MEM_WIDTH), lambda i: (i, 0, 0, 0)),
                   pl.BlockSpec((1, MEM_HEADS * MEM_VT_ROWS, n_mem), lambda i: (i, 0, 0))],
        out_shape=[jax.ShapeDtypeStruct((bsz, MEM_HEADS, n_mem, MEM_WIDTH), BF16),
                   jax.ShapeDtypeStruct((bsz, MEM_HEADS * MEM_VT_ROWS, n_mem), BF16)],
        compiler_params=_params(1),
        name="mem_kv",
    )(mem2d, w_kv)


def _proj_body(x_ref, pos_ref, invf_ref, w_ref, gmg_ref, gmb_ref, ws_ref, bst_ref, km_ref, vmt_ref,
               q_ref, k_ref, vt_ref, ygm_ref, ymem_ref):
    tm = x_ref.shape[0]
    xb = x_ref[...].astype(BF16)

    def project(off, width):
        return jnp.dot(xb, w_ref[:, off:off + width], preferred_element_type=F32)

    h_qm = project(OFF_QM, MEM_WIDTH)
    h_uv = project(OFF_U, 2 * GM_WIDTH)

    qm = (h_qm * (MEM_HEAD_DIM ** -0.5 * math.log2(math.e))).astype(BF16)
    st = [lax.dot_general(km_ref[0, hh], qm, NT_DIMS, preferred_element_type=F32) for hh in range(MEM_HEADS)]
    h_qk = project(OFF_Q, 2 * DIFF_Q_WIDTH)
    e = [jnp.exp2(s - jnp.max(s, axis=0, keepdims=True)).astype(BF16) for s in st]
    outs = []
    for hh in range(MEM_HEADS):
        ot = jnp.dot(vmt_ref[0, hh * MEM_VT_ROWS:(hh + 1) * MEM_VT_ROWS], e[hh], preferred_element_type=F32)
        outs.append(ot[:MEM_HEAD_DIM] * (1.0 / ot[MEM_HEAD_DIM:MEM_HEAD_DIM + 1]))
    ymem_ref[...] = jnp.concatenate(outs, axis=0).T.astype(BF16)

    u = jax.nn.gelu(h_uv[:, :GM_WIDTH])
    v = jax.nn.gelu(h_uv[:, GM_WIDTH:])
    vn = _layer_norm(v, gmg_ref[...], gmb_ref[...]).astype(BF16)
    row = lax.broadcasted_iota(jnp.int32, (GM_CHUNK, GM_CHUNK), 0)
    col = lax.broadcasted_iota(jnp.int32, (GM_CHUNK, GM_CHUNK), 1)
    causal = col <= row
    for g in range(GM_GROUPS):
        wc = jnp.where(causal, ws_ref[g], 0.0).astype(BF16)
        bcol = bst_ref[:, g:g + 1]
        gs = slice(g * GM_GROUP_DIM, (g + 1) * GM_GROUP_DIM)
        for c in range(tm // GM_CHUNK):
            rs = slice(c * GM_CHUNK, (c + 1) * GM_CHUNK)
            mixed = jnp.dot(wc, vn[rs, gs], preferred_element_type=F32) + bcol
            ygm_ref[rs, gs] = (u[rs, gs] * mixed).astype(BF16)
    h_vd = project(OFF_VD, DIFF_V_WIDTH)

    ang = invf_ref[...] * pos_ref[0].astype(F32)
    cos_t = jnp.cos(ang)
    sin_t = jnp.sin(ang)
    rest = DIFF_HEAD_DIM - ROPE_DIM
    one_t = jnp.ones((rest, tm), F32)
    zero_t = jnp.zeros((rest, tm), F32)
    zero_h = jnp.zeros((ROPE_HALF, tm), F32)
    cos = jnp.concatenate([cos_t, cos_t, one_t] * 2, axis=0).T
    sin_lo = jnp.concatenate([-sin_t, zero_h, zero_t] * 2, axis=0).T
    sin_hi = jnp.concatenate([zero_h, sin_t, zero_t] * 2, axis=0).T

    def rope(t):
        return (t * cos + pltpu.roll(t, LANES - ROPE_HALF, axis=1) * sin_lo
                + pltpu.roll(t, ROPE_HALF, axis=1) * sin_hi)

    scale = DIFF_HEAD_DIM ** -0.5 * math.log2(math.e)
    for j in range(DIFF_Q_WIDTH // LANES):
        cs = slice(j * LANES, (j + 1) * LANES)
        q_ref[:, cs] = (rope(h_qk[:, j * LANES:(j + 1) * LANES]) * scale).astype(BF16)
        k_ref[:, cs] = rope(h_qk[:, DIFF_Q_WIDTH + j * LANES:DIFF_Q_WIDTH + (j + 1) * LANES]).astype(BF16)
    vt = h_vd.T.astype(BF16)
    ones = jnp.ones((VT_ROWS - DIFF_V_DIM, tm), BF16)
    for hh in range(DIFF_HEADS):
        vt_ref[0, hh * VT_ROWS:hh * VT_ROWS + DIFF_V_DIM] = vt[hh * DIFF_V_DIM:(hh + 1) * DIFF_V_DIM]
        vt_ref[0, hh * VT_ROWS + DIFF_V_DIM:(hh + 1) * VT_ROWS] = ones


def _proj(x2d, pos3d, invf, w, gm_g, gm_b, w_s, b_st, km, vmt, seq, n_mem):
    t = x2d.shape[0]
    tiles_per_batch = seq // TM_PROJ
    row_spec = lambda width: pl.BlockSpec((TM_PROJ, width), lambda i: (i, 0))
    return pl.pallas_call(
        _proj_body,
        grid=(t // TM_PROJ,),
        in_specs=[
            row_spec(D_MODEL),
            pl.BlockSpec((1, 1, TM_PROJ), lambda i: (i, 0, 0)),
            _const_spec((ROPE_HALF, 1)),
            _const_spec((D_MODEL, IN_WIDTH)),
            _const_spec((1, GM_WIDTH)),
            _const_spec((1, GM_WIDTH)),
            _const_spec((GM_GROUPS, GM_CHUNK, GM_CHUNK)),
            _const_spec((GM_CHUNK, GM_GROUPS)),
            pl.BlockSpec((1, MEM_HEADS, n_mem, MEM_WIDTH), lambda i: (i // tiles_per_batch, 0, 0, 0)),
            pl.BlockSpec((1, MEM_HEADS * MEM_VT_ROWS, n_mem), lambda i: (i // tiles_per_batch, 0, 0)),
        ],
        out_specs=[row_spec(DIFF_Q_WIDTH), row_spec(DIFF_Q_WIDTH),
                   pl.BlockSpec((1, DIFF_HEADS * VT_ROWS, TM_PROJ),
                                lambda i: (i // tiles_per_batch, 0, i % tiles_per_batch)),
                   row_spec(GM_WIDTH), row_spec(MEM_WIDTH)],
        out_shape=[jax.ShapeDtypeStruct((t, DIFF_Q_WIDTH), BF16),
                   jax.ShapeDtypeStruct((t, DIFF_Q_WIDTH), BF16),
                   jax.ShapeDtypeStruct((t // seq, DIFF_HEADS * VT_ROWS, seq), BF16),
                   jax.ShapeDtypeStruct((t, GM_WIDTH), BF16),
                   jax.ShapeDtypeStruct((t, MEM_WIDTH), BF16)],
        compiler_params=_params(1),
        name="proj",
    )(x2d, pos3d, invf, w, gm_g, gm_b, w_s, b_st, km, vmt)


def _diff_attn_body(lamv_ref, ngc_ref, q_ref, k_ref, vt_ref, o_ref, qs_ref, st_ref, e_ref, acc_ref):
    qi = pl.program_id(1)
    heads = range(DIFF_HEADS)
    lane = lax.broadcasted_iota(jnp.int32, (TQ, DIFF_V_DIM), 1)
    for h in heads:
        q = q_ref[0, :, h * DIFF_V_DIM:(h + 1) * DIFF_V_DIM]
        zero = jnp.zeros_like(q)
        qs_ref[h, :TQ] = jnp.where(lane < DIFF_HEAD_DIM, q, zero)
        qs_ref[h, TQ:] = jnp.where(lane >= DIFF_HEAD_DIM, q, zero)
    acc_ref[...] = jnp.zeros(acc_ref.shape, F32)
    e_ref[...] = jnp.zeros(e_ref.shape, BF16)

    def scores(h, ki):
        kb = k_ref[0, pl.ds(pl.multiple_of(ki * TQ, TQ), TQ), h * DIFF_V_DIM:(h + 1) * DIFF_V_DIM]
        st = lax.dot_general(kb, qs_ref[h], NT_DIMS, preferred_element_type=F32)
        st_ref[h] = st
        return jnp.max(st, axis=0, keepdims=True)

    def values(h, ki):
        vtb = vt_ref[0, h * VT_ROWS:(h + 1) * VT_ROWS, pl.ds(pl.multiple_of(ki * TQ, TQ), TQ)]
        return jnp.dot(vtb, e_ref[h], preferred_element_type=F32)

    def softmax(h, st, m, bm):
        m_new = jnp.maximum(m, bm)
        e_ref[h] = jnp.exp2(st - m_new).astype(BF16)
        return jnp.exp2(m - m_new), m_new

    def step(t, carry):
        a_prev, m, bm = carry
        pv = [values(h, jnp.maximum(t - 1, 0)) for h in heads]
        am = [softmax(h, st_ref[h], m[h], bm[h]) for h in heads]
        for h in heads:
            acc_ref[h] = a_prev[h] * acc_ref[h] + pv[h]
        return (tuple(x[0] for x in am), tuple(x[1] for x in am),
                tuple(scores(h, t + 1) for h in heads))

    init = (tuple(jnp.ones((1, 2 * TQ), F32) for _ in heads),
            tuple(jnp.full((1, 2 * TQ), -jnp.inf, F32) for _ in heads),
            tuple(scores(h, 0) for h in heads))
    a_prev, m, _ = lax.fori_loop(0, qi, step, init)

    lv = lamv_ref[...]
    lam = (jnp.exp(jnp.sum(lv[0:1] * lv[1:2], axis=-1, keepdims=True))
           - jnp.exp(jnp.sum(lv[2:3] * lv[3:4], axis=-1, keepdims=True)) + LAM_INIT)
    kpos = lax.broadcasted_iota(jnp.int32, (TQ, 2 * TQ), 0)
    qpos = lax.broadcasted_iota(jnp.int32, (TQ, 2 * TQ), 1) % TQ
    visible = kpos <= qpos
    pv = [values(h, jnp.maximum(qi - 1, 0)) for h in heads]
    for h in heads:
        st = jnp.where(visible, st_ref[h], -jnp.inf)
        a, _ = softmax(h, st, m[h], jnp.max(st, axis=0, keepdims=True))
        acc = a * (a_prev[h] * acc_ref[h] + pv[h]) + values(h, qi)
        on = acc[:DIFF_V_DIM] * (1.0 / acc[DIFF_V_DIM:DIFF_V_DIM + 1])
        ot = on[:, :TQ] - lam * on[:, TQ:]
        ot = ot * lax.rsqrt(jnp.mean(ot * ot, axis=0, keepdims=True) + LN_EPS) * ngc_ref[...]
        o_ref[0, :, h * DIFF_V_DIM:(h + 1) * DIFF_V_DIM] = (ot * (1.0 - LAM_INIT)).T.astype(BF16)


def _diff_attn(lamv, norm_g_col, q, k, vt):
    b, s, _ = q.shape
    return pl.pallas_call(
        _diff_attn_body,
        grid=(b, s // TQ),
        in_specs=[
            _const_spec((4, DIFF_HEAD_DIM)),
            _const_spec((DIFF_V_DIM, 1)),
            pl.BlockSpec((1, TQ, DIFF_Q_WIDTH), lambda bi, qi: (bi, qi, 0)),
            pl.BlockSpec((1, s, DIFF_Q_WIDTH), lambda bi, qi: (bi, 0, 0)),
            pl.BlockSpec((1, DIFF_HEADS * VT_ROWS, s), lambda bi, qi: (bi, 0, 0)),
        ],
        out_specs=pl.BlockSpec((1, TQ, DIFF_V_WIDTH), lambda bi, qi: (bi, qi, 0)),
        out_shape=jax.ShapeDtypeStruct((b, s, DIFF_V_WIDTH), BF16),
        scratch_shapes=[pltpu.VMEM((DIFF_HEADS, 2 * TQ, DIFF_V_DIM), BF16),
                        pltpu.VMEM((DIFF_HEADS, TQ, 2 * TQ), F32),
                        pltpu.VMEM((DIFF_HEADS, TQ, 2 * TQ), BF16),
                        pltpu.VMEM((DIFF_HEADS, VT_ROWS, 2 * TQ), F32)],
        compiler_params=_params(2),
        name="diff_attn",
    )(lamv, norm_g_col, q, k, vt)


def _merge_body(x_ref, ygm_ref, ydf_ref, ymm_ref, wgate_ref, gb_ref, wgm_ref, wdf_ref, wmm_ref, wo_ref,
                g_ref, b_ref, o_ref, z_ref, m_ref):
    def tile(zero_rows):
        x = x_ref[...]
        xb = x.astype(BF16)
        branch_refs = ((ygm_ref, wgm_ref), (ydf_ref, wdf_ref), (ymm_ref, wmm_ref))
        for c in range(D_MODEL // MXU_COLS):
            cs = slice(c * MXU_COLS, (c + 1) * MXU_COLS)
            merged = zero_rows[c]
            for r, (y_ref, w_ref) in enumerate(branch_refs):
                gs = slice(r * D_MODEL + c * MXU_COLS, r * D_MODEL + (c + 1) * MXU_COLS)
                y = jnp.dot(y_ref[...], w_ref[:, cs], preferred_element_type=F32)
                logits = jnp.dot(xb, wgate_ref[:, OFF_GATE + gs.start:OFF_GATE + gs.stop],
                                 preferred_element_type=F32)
                merged = merged + (y + jnp.tanh(logits + gb_ref[:, gs]) * y)
            m_ref[:, cs] = merged.astype(BF16)
        y = jnp.dot(m_ref[...], wo_ref[...], preferred_element_type=F32)
        return DEEPNORM_ALPHA * x + y

    _run_lagged(tile, z_ref, g_ref, b_ref, o_ref)


def _merge(x2d, ygm, ydf, ymm, w_gate, gate_b, w_gm, w_df, w_mm, w_o, g, b):
    t = x2d.shape[0]
    n_tiles = t // TM_MERGE
    row_spec, out_spec = _lagged_row_specs(n_tiles, TM_MERGE)
    return pl.pallas_call(
        _merge_body,
        grid=(n_tiles + 1,),
        in_specs=[
            row_spec(D_MODEL), row_spec(GM_WIDTH), row_spec(DIFF_V_WIDTH), row_spec(MEM_WIDTH),
            _const_spec((D_MODEL, IN_WIDTH)),
            _const_spec((1, N_BRANCH * D_MODEL)),
            _const_spec((GM_WIDTH, D_MODEL)),
            _const_spec((DIFF_V_WIDTH, D_MODEL)),
            _const_spec((MEM_WIDTH, D_MODEL)),
            _const_spec((D_MODEL, D_MODEL)),
            _const_spec((1, D_MODEL)),
            _const_spec((1, D_MODEL)),
        ],
        out_specs=out_spec(D_MODEL),
        out_shape=jax.ShapeDtypeStruct((t, D_MODEL), F32),
        scratch_shapes=[pltpu.VMEM((TM_MERGE, D_MODEL), F32), pltpu.VMEM((TM_MERGE, D_MODEL), BF16)],
        compiler_params=_params_sequential(),
        name="merge",
    )(x2d, ygm, ydf, ymm, w_gate, gate_b, w_gm, w_df, w_mm, w_o, g, b)


def _rope_freqs():
    return (ROPE_THETA ** (-jnp.arange(0, ROPE_DIM, 2, dtype=F32) / ROPE_DIM)).reshape(ROPE_HALF, 1)


def kernel(x, mem, positions, ffn1_w_in, ffn1_w_out, ln1_g, ln1_b, w_in, gate_b, gm_ln_g, gm_ln_b, gm_w_s, gm_b_s,
           lambda_q1, lambda_k1, lambda_q2, lambda_k2, diff_norm_g, w_mem_kv, w_branch_gm, w_branch_diff,
           w_branch_mem, w_o, ln2_g, ln2_b, ffn2_w_in, ffn2_w_out, ln3_g, ln3_b):
    bsz, seq, _ = x.shape
    n_mem = mem.shape[1]
    t = bsz * seq
    i = 0
    x2d = x.reshape(t, D_MODEL)

    ffn_scale = jnp.concatenate([jnp.full((1, D_FF), 0.5, F32), jnp.ones((1, D_FF), F32)], axis=1)
    ffn_w_in = lambda w: (w * ffn_scale).astype(BF16)
    in_scale = jnp.concatenate([jnp.ones((1, OFF_GATE), F32), jnp.full((1, N_BRANCH * D_MODEL), 0.5, F32)], axis=1)
    w_in_b = (w_in[i] * in_scale).astype(BF16)

    x1 = _ffn_ln(x2d, ffn_w_in(ffn1_w_in[i]), ffn1_w_out[i].astype(BF16), ln1_g[i][None], ln1_b[i][None])

    km, vmt = _mem_kv(mem.reshape(bsz * n_mem, D_MODEL), w_mem_kv[i].astype(BF16), n_mem)
    q, k, vt, ygm, ymm = _proj(
        x1, positions.reshape(t // TM_PROJ, 1, TM_PROJ), _rope_freqs(), w_in_b,
        gm_ln_g[i][None], gm_ln_b[i][None], gm_w_s[i], gm_b_s[i].T, km, vmt, seq, n_mem)

    lamv = jnp.stack([lambda_q1[i], lambda_k1[i], lambda_q2[i], lambda_k2[i]]).astype(F32)
    ydf = _diff_attn(lamv, diff_norm_g[i][:, None],
                     q.reshape(bsz, seq, DIFF_Q_WIDTH), k.reshape(bsz, seq, DIFF_Q_WIDTH),
                     vt).reshape(t, DIFF_V_WIDTH)

    x2 = _merge(x1, ygm, ydf, ymm, w_in_b, 0.5 * gate_b[i][None],
                w_branch_gm[i].astype(BF16), w_branch_diff[i].astype(BF16), w_branch_mem[i].astype(BF16),
                (0.5 * w_o[i]).astype(BF16), ln2_g[i][None], ln2_b[i][None])

    x3 = _ffn_ln(x2, ffn_w_in(ffn2_w_in[i]), ffn2_w_out[i].astype(BF16), ln3_g[i][None], ln3_b[i][None])
    return x3.reshape(bsz, seq, D_MODEL)
```

```python
import functools
import math

import jax
import jax.numpy as jnp
from jax import lax
from jax.experimental import pallas as pl
from jax.experimental.pallas import tpu as pltpu

F32 = jnp.float32
BF16 = jnp.bfloat16

D_MODEL = 1024
D_FF = 2816
GM_WIDTH = 512
GM_GROUPS = 4
GM_GROUP_DIM = GM_WIDTH // GM_GROUPS
GM_CHUNK = 128
DIFF_HEADS = 4
DIFF_HEAD_DIM = 64
DIFF_V_DIM = 2 * DIFF_HEAD_DIM
DIFF_Q_WIDTH = DIFF_HEADS * 2 * DIFF_HEAD_DIM
DIFF_V_WIDTH = DIFF_HEADS * DIFF_V_DIM
MEM_HEADS = 4
MEM_HEAD_DIM = 64
MEM_WIDTH = MEM_HEADS * MEM_HEAD_DIM
N_BRANCH = 3
ROPE_THETA = 500000.0
ROPE_DIM = DIFF_HEAD_DIM // 4
ROPE_HALF = ROPE_DIM // 2
DEPTH = 1
DEEPNORM_ALPHA = (2 * DEPTH) ** 0.25
LN_EPS = 1e-5
LAM_INIT = 0.8 - 0.6 * math.exp(-0.3 * 0)
OFF_U = 0
OFF_V = OFF_U + GM_WIDTH
OFF_Q = OFF_V + GM_WIDTH
OFF_K = OFF_Q + DIFF_Q_WIDTH
OFF_VD = OFF_K + DIFF_Q_WIDTH
OFF_QM = OFF_VD + DIFF_V_WIDTH
OFF_GATE = OFF_QM + MEM_WIDTH
IN_WIDTH = OFF_GATE + N_BRANCH * D_MODEL

LANES = 128
BF16_SUBLANES = 16
VT_ROWS = DIFF_V_DIM + BF16_SUBLANES
MEM_VT_ROWS = MEM_HEAD_DIM + BF16_SUBLANES
MXU_COLS = 256
VMEM_LIMIT = 52 * 1024 * 1024

TM_FFN = 1024
TM_PROJ = 512
TM_MERGE = 1024
TQ = 512

NT_DIMS = (((1,), (1,)), ((), ()))


def _const_spec(shape):
    nd = len(shape)
    return pl.BlockSpec(shape, lambda *_: (0,) * nd, pipeline_mode=pl.Buffered(1))


def _params(n_axes):
    return pltpu.CompilerParams(dimension_semantics=("parallel",) * n_axes,
                                vmem_limit_bytes=VMEM_LIMIT)


def _params_sequential():
    return pltpu.CompilerParams(dimension_semantics=("arbitrary",), vmem_limit_bytes=VMEM_LIMIT)


def _layer_norm(z, g, b):
    mu = jnp.mean(z, axis=-1, keepdims=True)
    zc = z - mu
    var = jnp.mean(zc * zc, axis=-1, keepdims=True)
    return zc * lax.rsqrt(var + LN_EPS) * g + b


def _lagged_row_specs(n_tiles, tm):
    in_spec = lambda width: pl.BlockSpec((tm, width), lambda i: (jnp.minimum(i, n_tiles - 1), 0))
    out_spec = lambda width: pl.BlockSpec((tm, width), lambda i: (jnp.maximum(i - 1, 0), 0))
    return in_spec, out_spec


def _lagged_norm(z_ref, g_ref, b_ref, o_ref):
    n_groups = z_ref.shape[1] // MXU_COLS
    rows = z_ref.shape[0] // n_groups
    zero_rows = []
    for j in range(n_groups):
        rs = slice(j * rows, (j + 1) * rows)
        out = _layer_norm(z_ref[rs, :], g_ref[...], b_ref[...])
        o_ref[rs, :] = out
        bits = lax.bitcast_convert_type(jnp.max(out, axis=0, keepdims=True)[:, :MXU_COLS], jnp.uint32)
        zero_rows.append(lax.bitcast_convert_type((bits >> 16) >> 16, F32))
    return zero_rows


def _run_lagged(tile_fn, z_ref, g_ref, b_ref, o_ref):
    step = pl.program_id(0)
    last = pl.num_programs(0) - 1

    @pl.when(step == 0)
    def _():
        z_ref[...] = jnp.zeros(z_ref.shape, F32)

    @pl.when(step < last)
    def _():
        zero_rows = _lagged_norm(z_ref, g_ref, b_ref, o_ref)
        z_ref[...] = tile_fn(zero_rows)

    @pl.when(step == last)
    def _():
        _lagged_norm(z_ref, g_ref, b_ref, o_ref)


def _ffn_ln_body(n_casts, x_ref, win_ref, wout_ref, g_ref, b_ref, *refs):
    cast_in, o_ref, cast_out = refs[:2 * n_casts], refs[2 * n_casts], refs[2 * n_casts + 1:3 * n_casts + 1]
    z_ref, h_ref = refs[3 * n_casts + 1:]

    def tile(zero_rows):
        for j in range(n_casts):
            cast_out[j][...] = (cast_in[2 * j][...] * cast_in[2 * j + 1][...]).astype(BF16)
        x = x_ref[...]
        xb = x.astype(BF16)
        for c in range(D_FF // MXU_COLS):
            cs = slice(c * MXU_COLS, (c + 1) * MXU_COLS)
            g = jnp.dot(xb, win_ref[:, cs], preferred_element_type=F32)
            u = jnp.dot(xb, win_ref[:, D_FF + c * MXU_COLS:D_FF + (c + 1) * MXU_COLS], preferred_element_type=F32)
            if c < len(zero_rows):
                g = g + zero_rows[c]
            h_ref[:, cs] = ((g + g * jnp.tanh(g)) * u).astype(BF16)
        y = jnp.dot(h_ref[...], wout_ref[...], preferred_element_type=F32)
        return DEEPNORM_ALPHA * x + 0.5 * y

    _run_lagged(tile, z_ref, g_ref, b_ref, o_ref)


def _ffn_ln(x2d, w_in, w_out, g, b, casts=()):
    t = x2d.shape[0]
    n_tiles = t // TM_FFN
    in_spec, out_spec = _lagged_row_specs(n_tiles, TM_FFN)
    cast_specs, cast_args, cast_out_specs, cast_out_shapes = [], [], [], []
    for src, scale in casts:
        rows, cols = src.shape
        blk = rows // n_tiles
        assert blk * n_tiles == rows and blk % BF16_SUBLANES == 0, src.shape
        spec = pl.BlockSpec((blk, cols), lambda i: (jnp.minimum(i, n_tiles - 1), 0))
        cast_specs += [spec, _const_spec((1, cols))]
        cast_args += [src, scale]
        cast_out_specs.append(spec)
        cast_out_shapes.append(jax.ShapeDtypeStruct((rows, cols), BF16))
    return pl.pallas_call(
        functools.partial(_ffn_ln_body, len(casts)),
        grid=(n_tiles + 1,),
        in_specs=[
            in_spec(D_MODEL),
            _const_spec((D_MODEL, 2 * D_FF)),
            _const_spec((D_FF, D_MODEL)),
            _const_spec((1, D_MODEL)),
            _const_spec((1, D_MODEL)),
        ] + cast_specs,
        out_specs=[out_spec(D_MODEL)] + cast_out_specs,
        out_shape=[jax.ShapeDtypeStruct((t, D_MODEL), F32)] + cast_out_shapes,
        scratch_shapes=[pltpu.VMEM((TM_FFN, D_MODEL), F32), pltpu.VMEM((TM_FFN, D_FF), BF16)],
        compiler_params=_params_sequential(),
        name="ffn_ln",
    )(x2d, w_in, w_out, g, b, *cast_args)


def _mem_kv_body(m_ref, w_ref, km_ref, vmt_ref):
    n_mem = m_ref.shape[0]
    kv = jnp.dot(m_ref[...].astype(BF16), w_ref[...], preferred_element_type=F32)
    k = kv[:, :MEM_WIDTH].astype(BF16)
    vt = kv[:, MEM_WIDTH:].T.astype(BF16)
    lane = lax.broadcasted_iota(jnp.int32, k.shape, 1)
    ones = jnp.ones((MEM_VT_ROWS - MEM_HEAD_DIM, n_mem), BF16)
    for hh in range(MEM_HEADS):
        km_ref[0, hh] = jnp.where(lane // MEM_HEAD_DIM == hh, k, jnp.zeros_like(k))
        vmt_ref[0, hh * MEM_VT_ROWS:hh * MEM_VT_ROWS + MEM_HEAD_DIM] = vt[hh * MEM_HEAD_DIM:(hh + 1) * MEM_HEAD_DIM]
        vmt_ref[0, hh * MEM_VT_ROWS + MEM_HEAD_DIM:(hh + 1) * MEM_VT_ROWS] = ones


def _mem_kv(mem2d, w_kv, n_mem):
    bsz = mem2d.shape[0] // n_mem
    return pl.pallas_call(
        _mem_kv_body,
        grid=(bsz,),
        in_specs=[
            pl.BlockSpec((n_mem, D_MODEL), lambda i: (i, 0)),
            _const_spec((D_MODEL, 2 * MEM_WIDTH)),
        ],
        out_specs=[pl.BlockSpec((1, MEM_HEADS, n_mem, MEM_WIDTH), lambda i: (i, 0, 0, 0)),
                   pl.BlockSpec((1, MEM_HEADS * MEM_VT_ROWS, n_mem), lambda i: (i, 0, 0))],
        out_shape=[jax.ShapeDtypeStruct((bsz, MEM_HEADS, n_mem, MEM_WIDTH), BF16),
                   jax.ShapeDtypeStruct((bsz, MEM_HEADS * MEM_VT_ROWS, n_mem), BF16)],
        compiler_params=_params(1),
        name="mem_kv",
    )(mem2d, w_kv)


def _proj_body(x_ref, pos_ref, invf_ref, w_ref, gmg_ref, gmb_ref, ws_ref, bst_ref, km_ref, vmt_ref,
               q_ref, k_ref, vt_ref, ygm_ref, ymem_ref):
    tm = x_ref.shape[0]
    xb = x_ref[...].astype(BF16)

    def project(off, width):
        return jnp.dot(xb, w_ref[:, off:off + width], preferred_element_type=F32)

    h_qm = project(OFF_QM, MEM_WIDTH)
    h_uv = project(OFF_U, 2 * GM_WIDTH)

    qm = (h_qm * (MEM_HEAD_DIM ** -0.5 * math.log2(math.e))).astype(BF16)
    st = [lax.dot_general(km_ref[0, hh], qm, NT_DIMS, preferred_element_type=F32) for hh in range(MEM_HEADS)]
    h_qk = project(OFF_Q, 2 * DIFF_Q_WIDTH)
    e = [jnp.exp2(s - jnp.max(s, axis=0, keepdims=True)).astype(BF16) for s in st]
    outs = []
    for hh in range(MEM_HEADS):
        ot = jnp.dot(vmt_ref[0, hh * MEM_VT_ROWS:(hh + 1) * MEM_VT_ROWS], e[hh], preferred_element_type=F32)
        outs.append(ot[:MEM_HEAD_DIM] * (1.0 / ot[MEM_HEAD_DIM:MEM_HEAD_DIM + 1]))
    ymem_ref[...] = jnp.concatenate(outs, axis=0).T.astype(BF16)

    u = jax.nn.gelu(h_uv[:, :GM_WIDTH])
    v = jax.nn.gelu(h_uv[:, GM_WIDTH:])
    vn = _layer_norm(v, gmg_ref[...], gmb_ref[...]).astype(BF16)
    row = lax.broadcasted_iota(jnp.int32, (GM_CHUNK, GM_CHUNK), 0)
    col = lax.broadcasted_iota(jnp.int32, (GM_CHUNK, GM_CHUNK), 1)
    causal = col <= row
    for g in range(GM_GROUPS):
        wc = jnp.where(causal, ws_ref[g], 0.0).astype(BF16)
        bcol = bst_ref[:, g:g + 1]
        gs = slice(g * GM_GROUP_DIM, (g + 1) * GM_GROUP_DIM)
        for c in range(tm // GM_CHUNK):
            rs = slice(c * GM_CHUNK, (c + 1) * GM_CHUNK)
            mixed = jnp.dot(wc, vn[rs, gs], preferred_element_type=F32) + bcol
            ygm_ref[rs, gs] = (u[rs, gs] * mixed).astype(BF16)
    h_vd = project(OFF_VD, DIFF_V_WIDTH)

    ang = invf_ref[...] * pos_ref[0].astype(F32)
    cos_t = jnp.cos(ang)
    sin_t = jnp.sin(ang)
    rest = DIFF_HEAD_DIM - ROPE_DIM
    one_t = jnp.ones((rest, tm), F32)
    zero_t = jnp.zeros((rest, tm), F32)
    zero_h = jnp.zeros((ROPE_HALF, tm), F32)
    cos = jnp.concatenate([cos_t, cos_t, one_t] * 2, axis=0).T
    sin_lo = jnp.concatenate([-sin_t, zero_h, zero_t] * 2, axis=0).T
    sin_hi = jnp.concatenate([zero_h, sin_t, zero_t] * 2, axis=0).T

    def rope(t):
        return (t * cos + pltpu.roll(t, LANES - ROPE_HALF, axis=1) * sin_lo
                + pltpu.roll(t, ROPE_HALF, axis=1) * sin_hi)

    scale = DIFF_HEAD_DIM ** -0.5 * math.log2(math.e)
    for j in range(DIFF_Q_WIDTH // LANES):
        cs = slice(j * LANES, (j + 1) * LANES)
        q_ref[:, cs] = (rope(h_qk[:, j * LANES:(j + 1) * LANES]) * scale).astype(BF16)
        k_ref[:, cs] = rope(h_qk[:, DIFF_Q_WIDTH + j * LANES:DIFF_Q_WIDTH + (j + 1) * LANES]).astype(BF16)
    vt = h_vd.T.astype(BF16)
    ones = jnp.ones((VT_ROWS - DIFF_V_DIM, tm), BF16)
    for hh in range(DIFF_HEADS):
        vt_ref[0, hh * VT_ROWS:hh * VT_ROWS + DIFF_V_DIM] = vt[hh * DIFF_V_DIM:(hh + 1) * DIFF_V_DIM]
        vt_ref[0, hh * VT_ROWS + DIFF_V_DIM:(hh + 1) * VT_ROWS] = ones


def _proj(x2d, pos3d, invf, w, gm_g, gm_b, w_s, b_st, km, vmt, seq, n_mem):
    t = x2d.shape[0]
    tiles_per_batch = seq // TM_PROJ
    row_spec = lambda width: pl.BlockSpec((TM_PROJ, width), lambda i: (i, 0))
    return pl.pallas_call(
        _proj_body,
        grid=(t // TM_PROJ,),
        in_specs=[
            row_spec(D_MODEL),
            pl.BlockSpec((1, 1, TM_PROJ), lambda i: (i, 0, 0)),
            _const_spec((ROPE_HALF, 1)),
            _const_spec((D_MODEL, IN_WIDTH)),
            _const_spec((1, GM_WIDTH)),
            _const_spec((1, GM_WIDTH)),
            _const_spec((GM_GROUPS, GM_CHUNK, GM_CHUNK)),
            _const_spec((GM_CHUNK, GM_GROUPS)),
            pl.BlockSpec((1, MEM_HEADS, n_mem, MEM_WIDTH), lambda i: (i // tiles_per_batch, 0, 0, 0)),
            pl.BlockSpec((1, MEM_HEADS * MEM_VT_ROWS, n_mem), lambda i: (i // tiles_per_batch, 0, 0)),
        ],
        out_specs=[row_spec(DIFF_Q_WIDTH), row_spec(DIFF_Q_WIDTH),
                   pl.BlockSpec((1, DIFF_HEADS * VT_ROWS, TM_PROJ),
                                lambda i: (i // tiles_per_batch, 0, i % tiles_per_batch)),
                   row_spec(GM_WIDTH), row_spec(MEM_WIDTH)],
        out_shape=[jax.ShapeDtypeStruct((t, DIFF_Q_WIDTH), BF16),
                   jax.ShapeDtypeStruct((t, DIFF_Q_WIDTH), BF16),
                   jax.ShapeDtypeStruct((t // seq, DIFF_HEADS * VT_ROWS, seq), BF16),
                   jax.ShapeDtypeStruct((t, GM_WIDTH), BF16),
                   jax.ShapeDtypeStruct((t, MEM_WIDTH), BF16)],
        compiler_params=_params(1),
        name="proj",
    )(x2d, pos3d, invf, w, gm_g, gm_b, w_s, b_st, km, vmt)


def _diff_attn_body(lamv_ref, ngc_ref, q_ref, k_ref, vt_ref, o_ref, qs_ref, st_ref, e_ref, acc_ref):
    qi = pl.program_id(1)
    heads = range(DIFF_HEADS)
    lane = lax.broadcasted_iota(jnp.int32, (TQ, DIFF_V_DIM), 1)
    for h in heads:
        q = q_ref[0, :, h * DIFF_V_DIM:(h + 1) * DIFF_V_DIM]
        zero = jnp.zeros_like(q)
        qs_ref[h, :TQ] = jnp.where(lane < DIFF_HEAD_DIM, q, zero)
        qs_ref[h, TQ:] = jnp.where(lane >= DIFF_HEAD_DIM, q, zero)
    acc_ref[...] = jnp.zeros(acc_ref.shape, F32)
    e_ref[...] = jnp.zeros(e_ref.shape, BF16)

    def scores(h, ki):
        kb = k_ref[0, pl.ds(pl.multiple_of(ki * TQ, TQ), TQ), h * DIFF_V_DIM:(h + 1) * DIFF_V_DIM]
        st = lax.dot_general(kb, qs_ref[h], NT_DIMS, preferred_element_type=F32)
        st_ref[h] = st
        return jnp.max(st, axis=0, keepdims=True)

    def values(h, ki):
        vtb = vt_ref[0, h * VT_ROWS:(h + 1) * VT_ROWS, pl.ds(pl.multiple_of(ki * TQ, TQ), TQ)]
        return jnp.dot(vtb, e_ref[h], preferred_element_type=F32)

    def softmax(h, st, m, bm):
        m_new = jnp.maximum(m, bm)
        e_ref[h] = jnp.exp2(st - m_new).astype(BF16)
        return jnp.exp2(m - m_new), m_new

    def step(t, carry):
        a_prev, m, bm = carry
        pv = [values(h, jnp.maximum(t - 1, 0)) for h in heads]
        am = [softmax(h, st_ref[h], m[h], bm[h]) for h in heads]
        for h in heads:
            acc_ref[h] = a_prev[h] * acc_ref[h] + pv[h]
        return (tuple(x[0] for x in am), tuple(x[1] for x in am),
                tuple(scores(h, t + 1) for h in heads))

    init = (tuple(jnp.ones((1, 2 * TQ), F32) for _ in heads),
            tuple(jnp.full((1, 2 * TQ), -jnp.inf, F32) for _ in heads),
            tuple(scores(h, 0) for h in heads))
    a_prev, m, _ = lax.fori_loop(0, qi, step, init)

    lv = lamv_ref[...]
    lam = (jnp.exp(jnp.sum(lv[0:1] * lv[1:2], axis=-1, keepdims=True))
           - jnp.exp(jnp.sum(lv[2:3] * lv[3:4], axis=-1, keepdims=True)) + LAM_INIT)
    kpos = lax.broadcasted_iota(jnp.int32, (TQ, 2 * TQ), 0)
    qpos = lax.broadcasted_iota(jnp.int32, (TQ, 2 * TQ), 1) % TQ
    visible = kpos <= qpos
    pv = [values(h, jnp.maximum(qi - 1, 0)) for h in heads]
    for h in heads:
        st = jnp.where(visible, st_ref[h], -jnp.inf)
        a, _ = softmax(h, st, m[h], jnp.max(st, axis=0, keepdims=True))
        acc = a * (a_prev[h] * acc_ref[h] + pv[h]) + values(h, qi)
        on = acc[:DIFF_V_DIM] * (1.0 / acc[DIFF_V_DIM:DIFF_V_DIM + 1])
        ot = on[:, :TQ] - lam * on[:, TQ:]
        ot = ot * lax.rsqrt(jnp.mean(ot * ot, axis=0, keepdims=True) + LN_EPS) * ngc_ref[...]
        o_ref[0, :, h * DIFF_V_DIM:(h + 1) * DIFF_V_DIM] = (ot * (1.0 - LAM_INIT)).T.astype(BF16)


def _diff_attn(lamv, norm_g_col, q, k, vt):
    b, s, _ = q.shape
    return pl.pallas_call(
        _diff_attn_body,
        grid=(b, s // TQ),
        in_specs=[
            _const_spec((4, DIFF_HEAD_DIM)),
            _const_spec((DIFF_V_DIM, 1)),
            pl.BlockSpec((1, TQ, DIFF_Q_WIDTH), lambda bi, qi: (bi, qi, 0)),
            pl.BlockSpec((1, s, DIFF_Q_WIDTH), lambda bi, qi: (bi, 0, 0)),
            pl.BlockSpec((1, DIFF_HEADS * VT_ROWS, s), lambda bi, qi: (bi, 0, 0)),
        ],
        out_specs=pl.BlockSpec((1, TQ, DIFF_V_WIDTH), lambda bi, qi: (bi, qi, 0)),
        out_shape=jax.ShapeDtypeStruct((b, s, DIFF_V_WIDTH), BF16),
        scratch_shapes=[pltpu.VMEM((DIFF_HEADS, 2 * TQ, DIFF_V_DIM), BF16),
                        pltpu.VMEM((DIFF_HEADS, TQ, 2 * TQ), F32),
                        pltpu.VMEM((DIFF_HEADS, TQ, 2 * TQ), BF16),
                        pltpu.VMEM((DIFF_HEADS, VT_ROWS, 2 * TQ), F32)],
        compiler_params=_params(2),
        name="diff_attn",
    )(lamv, norm_g_col, q, k, vt)


def _merge_body(x_ref, ygm_ref, ydf_ref, ymm_ref, wgate_ref, gb_ref, wgm_ref, wdf_ref, wmm_ref, wo_ref,
                g_ref, b_ref, o_ref, z_ref, m_ref):
    def tile(zero_rows):
        x = x_ref[...]
        xb = x.astype(BF16)
        branch_refs = ((ygm_ref, wgm_ref), (ydf_ref, wdf_ref), (ymm_ref, wmm_ref))
        for c in range(D_MODEL // MXU_COLS):
            cs = slice(c * MXU_COLS, (c + 1) * MXU_COLS)
            merged = zero_rows[c]
            for r, (y_ref, w_ref) in enumerate(branch_refs):
                gs = slice(r * D_MODEL + c * MXU_COLS, r * D_MODEL + (c + 1) * MXU_COLS)
                y = jnp.dot(y_ref[...], w_ref[:, cs], preferred_element_type=F32)
                logits = jnp.dot(xb, wgate_ref[:, OFF_GATE + gs.start:OFF_GATE + gs.stop],
                                 preferred_element_type=F32)
                merged = merged + (y + jnp.tanh(logits + gb_ref[:, gs]) * y)
            m_ref[:, cs] = merged.astype(BF16)
        y = jnp.dot(m_ref[...], wo_ref[...], preferred_element_type=F32)
        return DEEPNORM_ALPHA * x + y

    _run_lagged(tile, z_ref, g_ref, b_ref, o_ref)


def _merge(x2d, ygm, ydf, ymm, w_gate, gate_b, w_gm, w_df, w_mm, w_o, g, b):
    t = x2d.shape[0]
    n_tiles = t // TM_MERGE
    row_spec, out_spec = _lagged_row_specs(n_tiles, TM_MERGE)
    return pl.pallas_call(
        _merge_body,
        grid=(n_tiles + 1,),
        in_specs=[
            row_spec(D_MODEL), row_spec(GM_WIDTH), row_spec(DIFF_V_WIDTH), row_spec(MEM_WIDTH),
            _const_spec((D_MODEL, IN_WIDTH)),
            _const_spec((1, N_BRANCH * D_MODEL)),
            _const_spec((GM_WIDTH, D_MODEL)),
            _const_spec((DIFF_V_WIDTH, D_MODEL)),
            _const_spec((MEM_WIDTH, D_MODEL)),
            _const_spec((D_MODEL, D_MODEL)),
            _const_spec((1, D_MODEL)),
            _const_spec((1, D_MODEL)),
        ],
        out_specs=out_spec(D_MODEL),
        out_shape=jax.ShapeDtypeStruct((t, D_MODEL), F32),
        scratch_shapes=[pltpu.VMEM((TM_MERGE, D_MODEL), F32), pltpu.VMEM((TM_MERGE, D_MODEL), BF16)],
        compiler_params=_params_sequential(),
        name="merge",
    )(x2d, ygm, ydf, ymm, w_gate, gate_b, w_gm, w_df, w_mm, w_o, g, b)


def _rope_freqs():
    return (ROPE_THETA ** (-jnp.arange(0, ROPE_DIM, 2, dtype=F32) / ROPE_DIM)).reshape(ROPE_HALF, 1)


def kernel(x, mem, positions, ffn1_w_in, ffn1_w_out, ln1_g, ln1_b, w_in, gate_b, gm_ln_g, gm_ln_b, gm_w_s, gm_b_s,
           lambda_q1, lambda_k1, lambda_q2, lambda_k2, diff_norm_g, w_mem_kv, w_branch_gm, w_branch_diff,
           w_branch_mem, w_o, ln2_g, ln2_b, ffn2_w_in, ffn2_w_out, ln3_g, ln3_b):
    bsz, seq, _ = x.shape
    n_mem = mem.shape[1]
    t = bsz * seq
    i = 0
    x2d = x.reshape(t, D_MODEL)

    ffn_scale = jnp.concatenate([jnp.full((1, D_FF), 0.5, F32), jnp.ones((1, D_FF), F32)], axis=1)
    in_scale = jnp.concatenate([jnp.ones((1, OFF_GATE), F32), jnp.full((1, N_BRANCH * D_MODEL), 0.5, F32)], axis=1)

    w_out2_view = ffn2_w_out[i].reshape(-1, 2 * D_FF)
    x1, w_in_b, ffn2_w_in_b, ffn2_w_out_b = _ffn_ln(
        x2d, (ffn1_w_in[i] * ffn_scale).astype(BF16), ffn1_w_out[i].astype(BF16), ln1_g[i][None], ln1_b[i][None],
        casts=((w_in[i], in_scale), (ffn2_w_in[i], ffn_scale), (w_out2_view, jnp.ones((1, 2 * D_FF), F32))))

    km, vmt = _mem_kv(mem.reshape(bsz * n_mem, D_MODEL), w_mem_kv[i].astype(BF16), n_mem)
    q, k, vt, ygm, ymm = _proj(
        x1, positions.reshape(t // TM_PROJ, 1, TM_PROJ), _rope_freqs(), w_in_b,
        gm_ln_g[i][None], gm_ln_b[i][None], gm_w_s[i], gm_b_s[i].T, km, vmt, seq, n_mem)

    lamv = jnp.stack([lambda_q1[i], lambda_k1[i], lambda_q2[i], lambda_k2[i]]).astype(F32)
    ydf = _diff_attn(lamv, diff_norm_g[i][:, None],
                     q.reshape(bsz, seq, DIFF_Q_WIDTH), k.reshape(bsz, seq, DIFF_Q_WIDTH),
                     vt).reshape(t, DIFF_V_WIDTH)

    x2 = _merge(x1, ygm, ydf, ymm, w_in_b, 0.5 * gate_b[i][None],
                w_branch_gm[i].astype(BF16), w_branch_diff[i].astype(BF16), w_branch_mem[i].astype(BF16),
                (0.5 * w_o[i]).astype(BF16), ln2_g[i][None], ln2_b[i][None])

    x3, = _ffn_ln(x2, ffn2_w_in_b, ffn2_w_out_b.reshape(D_FF, D_MODEL), ln3_g[i][None], ln3_b[i][None])
    return x3.reshape(bsz, seq, D_MODEL)
```

```python
import functools
import math

import jax
import jax.numpy as jnp
from jax import lax
from jax.experimental import pallas as pl
from jax.experimental.pallas import tpu as pltpu

F32 = jnp.float32
BF16 = jnp.bfloat16

D_MODEL = 1024
D_FF = 2816
GM_WIDTH = 512
GM_GROUPS = 4
GM_GROUP_DIM = GM_WIDTH // GM_GROUPS
GM_CHUNK = 128
DIFF_HEADS = 4
DIFF_HEAD_DIM = 64
DIFF_V_DIM = 2 * DIFF_HEAD_DIM
DIFF_Q_WIDTH = DIFF_HEADS * 2 * DIFF_HEAD_DIM
DIFF_V_WIDTH = DIFF_HEADS * DIFF_V_DIM
MEM_HEADS = 4
MEM_HEAD_DIM = 64
MEM_WIDTH = MEM_HEADS * MEM_HEAD_DIM
N_BRANCH = 3
ROPE_THETA = 500000.0
ROPE_DIM = DIFF_HEAD_DIM // 4
ROPE_HALF = ROPE_DIM // 2
DEPTH = 1
DEEPNORM_ALPHA = (2 * DEPTH) ** 0.25
LN_EPS = 1e-5
LAM_INIT = 0.8 - 0.6 * math.exp(-0.3 * 0)
OFF_U = 0
OFF_V = OFF_U + GM_WIDTH
OFF_Q = OFF_V + GM_WIDTH
OFF_K = OFF_Q + DIFF_Q_WIDTH
OFF_VD = OFF_K + DIFF_Q_WIDTH
OFF_QM = OFF_VD + DIFF_V_WIDTH
OFF_GATE = OFF_QM + MEM_WIDTH
IN_WIDTH = OFF_GATE + N_BRANCH * D_MODEL

LANES = 128
BF16_SUBLANES = 16
VT_ROWS = DIFF_V_DIM + BF16_SUBLANES
MEM_VT_ROWS = MEM_HEAD_DIM + BF16_SUBLANES
MXU_COLS = 256
VMEM_LIMIT = 52 * 1024 * 1024

TM_FFN = 1024
TM_PROJ = 512
TM_MERGE = 1024
TQ = 512

NT_DIMS = (((1,), (1,)), ((), ()))


def _const_spec(shape):
    nd = len(shape)
    return pl.BlockSpec(shape, lambda *_: (0,) * nd, pipeline_mode=pl.Buffered(1))


def _params(n_axes):
    return pltpu.CompilerParams(dimension_semantics=("parallel",) * n_axes,
                                vmem_limit_bytes=VMEM_LIMIT)


def _params_sequential():
    return pltpu.CompilerParams(dimension_semantics=("arbitrary",), vmem_limit_bytes=VMEM_LIMIT)


def _layer_norm(z, g, b):
    mu = jnp.mean(z, axis=-1, keepdims=True)
    zc = z - mu
    var = jnp.mean(zc * zc, axis=-1, keepdims=True)
    return zc * lax.rsqrt(var + LN_EPS) * g + b


def _lagged_row_specs(n_tiles, tm):
    in_spec = lambda width: pl.BlockSpec((tm, width), lambda i: (jnp.minimum(i, n_tiles - 1), 0))
    out_spec = lambda width: pl.BlockSpec((tm, width), lambda i: (jnp.maximum(i - 1, 0), 0))
    return in_spec, out_spec


def _lagged_norm(z_ref, g_ref, b_ref, o_ref):
    n_groups = z_ref.shape[1] // MXU_COLS
    rows = z_ref.shape[0] // n_groups
    zero_rows = []
    for j in range(n_groups):
        rs = slice(j * rows, (j + 1) * rows)
        out = _layer_norm(z_ref[rs, :], g_ref[...], b_ref[...])
        o_ref[rs, :] = out
        bits = lax.bitcast_convert_type(jnp.max(out, axis=0, keepdims=True)[:, :MXU_COLS], jnp.uint32)
        zero_rows.append(lax.bitcast_convert_type((bits >> 16) >> 16, F32))
    return zero_rows


def _run_lagged(tile_fn, z_ref, g_ref, b_ref, o_ref):
    step = pl.program_id(0)
    last = pl.num_programs(0) - 1

    @pl.when(step == 0)
    def _():
        z_ref[...] = jnp.zeros(z_ref.shape, F32)

    @pl.when(step < last)
    def _():
        zero_rows = _lagged_norm(z_ref, g_ref, b_ref, o_ref)
        z_ref[...] = tile_fn(zero_rows)

    @pl.when(step == last)
    def _():
        _lagged_norm(z_ref, g_ref, b_ref, o_ref)


def _ffn_ln_body(n_casts, x_ref, win_ref, wout_ref, g_ref, b_ref, *refs):
    cast_in, o_ref, cast_out = refs[:2 * n_casts], refs[2 * n_casts], refs[2 * n_casts + 1:3 * n_casts + 1]
    z_ref, h_ref = refs[3 * n_casts + 1:]

    def tile(zero_rows):
        for j in range(n_casts):
            cast_out[j][...] = (cast_in[2 * j][0] * cast_in[2 * j + 1][...]).astype(BF16)
        x = x_ref[...]
        xb = x.astype(BF16)
        for c in range(D_FF // MXU_COLS):
            cs = slice(c * MXU_COLS, (c + 1) * MXU_COLS)
            g = jnp.dot(xb, win_ref[:, cs], preferred_element_type=F32)
            u = jnp.dot(xb, win_ref[:, D_FF + c * MXU_COLS:D_FF + (c + 1) * MXU_COLS], preferred_element_type=F32)
            if c < len(zero_rows):
                g = g + zero_rows[c]
            h_ref[:, cs] = ((g + g * jnp.tanh(g)) * u).astype(BF16)
        y = jnp.dot(h_ref[...], wout_ref[...], preferred_element_type=F32)
        return DEEPNORM_ALPHA * x + 0.5 * y

    _run_lagged(tile, z_ref, g_ref, b_ref, o_ref)


def _ffn_ln(x2d, w_in, w_out, g, b, casts=()):
    t = x2d.shape[0]
    n_tiles = t // TM_FFN
    in_spec, out_spec = _lagged_row_specs(n_tiles, TM_FFN)
    cast_specs, cast_args, cast_out_specs, cast_out_shapes = [], [], [], []
    for src, scale in casts:
        _, rows, cols = src.shape
        blk = next(r for r in range(BF16_SUBLANES, rows + 1, BF16_SUBLANES)
                   if rows % r == 0 and rows // r <= n_tiles)
        last = rows // blk - 1
        cast_specs += [pl.BlockSpec((1, blk, cols), lambda i, last=last: (0, jnp.minimum(i, last), 0)),
                       _const_spec((1, cols))]
        cast_args += [src, scale]
        cast_out_specs.append(pl.BlockSpec((blk, cols), lambda i, last=last: (jnp.minimum(i, last), 0)))
        cast_out_shapes.append(jax.ShapeDtypeStruct((rows, cols), BF16))
    return pl.pallas_call(
        functools.partial(_ffn_ln_body, len(casts)),
        grid=(n_tiles + 1,),
        in_specs=[
            in_spec(D_MODEL),
            _const_spec((D_MODEL, 2 * D_FF)),
            _const_spec((D_FF, D_MODEL)),
            _const_spec((1, D_MODEL)),
            _const_spec((1, D_MODEL)),
        ] + cast_specs,
        out_specs=[out_spec(D_MODEL)] + cast_out_specs,
        out_shape=[jax.ShapeDtypeStruct((t, D_MODEL), F32)] + cast_out_shapes,
        scratch_shapes=[pltpu.VMEM((TM_FFN, D_MODEL), F32), pltpu.VMEM((TM_FFN, D_FF), BF16)],
        compiler_params=_params_sequential(),
        name="ffn_ln",
    )(x2d, w_in, w_out, g, b, *cast_args)


def _mem_kv_body(m_ref, w_ref, km_ref, vmt_ref):
    n_mem = m_ref.shape[0]
    kv = jnp.dot(m_ref[...].astype(BF16), w_ref[...], preferred_element_type=F32)
    k = kv[:, :MEM_WIDTH].astype(BF16)
    vt = kv[:, MEM_WIDTH:].T.astype(BF16)
    lane = lax.broadcasted_iota(jnp.int32, k.shape, 1)
    ones = jnp.ones((MEM_VT_ROWS - MEM_HEAD_DIM, n_mem), BF16)
    for hh in range(MEM_HEADS):
        km_ref[0, hh] = jnp.where(lane // MEM_HEAD_DIM == hh, k, jnp.zeros_like(k))
        vmt_ref[0, hh * MEM_VT_ROWS:hh * MEM_VT_ROWS + MEM_HEAD_DIM] = vt[hh * MEM_HEAD_DIM:(hh + 1) * MEM_HEAD_DIM]
        vmt_ref[0, hh * MEM_VT_ROWS + MEM_HEAD_DIM:(hh + 1) * MEM_VT_ROWS] = ones


def _mem_kv(mem2d, w_kv, n_mem):
    bsz = mem2d.shape[0] // n_mem
    return pl.pallas_call(
        _mem_kv_body,
        grid=(bsz,),
        in_specs=[
            pl.BlockSpec((n_mem, D_MODEL), lambda i: (i, 0)),
            _const_spec((D_MODEL, 2 * MEM_WIDTH)),
        ],
        out_specs=[pl.BlockSpec((1, MEM_HEADS, n_mem, MEM_WIDTH), lambda i: (i, 0, 0, 0)),
                   pl.BlockSpec((1, MEM_HEADS * MEM_VT_ROWS, n_mem), lambda i: (i, 0, 0))],
        out_shape=[jax.ShapeDtypeStruct((bsz, MEM_HEADS, n_mem, MEM_WIDTH), BF16),
                   jax.ShapeDtypeStruct((bsz, MEM_HEADS * MEM_VT_ROWS, n_mem), BF16)],
        compiler_params=_params(1),
        name="mem_kv",
    )(mem2d, w_kv)


def _proj_body(x_ref, pos_ref, invf_ref, w_ref, gmg_ref, gmb_ref, ws_ref, bst_ref, km_ref, vmt_ref,
               q_ref, k_ref, vt_ref, ygm_ref, ymem_ref):
    tm = x_ref.shape[0]
    xb = x_ref[...].astype(BF16)

    def project(off, width):
        return jnp.dot(xb, w_ref[:, off:off + width], preferred_element_type=F32)

    h_qm = project(OFF_QM, MEM_WIDTH)
    h_uv = project(OFF_U, 2 * GM_WIDTH)

    qm = (h_qm * (MEM_HEAD_DIM ** -0.5 * math.log2(math.e))).astype(BF16)
    st = [lax.dot_general(km_ref[0, hh], qm, NT_DIMS, preferred_element_type=F32) for hh in range(MEM_HEADS)]
    h_qk = project(OFF_Q, 2 * DIFF_Q_WIDTH)
    e = [jnp.exp2(s - jnp.max(s, axis=0, keepdims=True)).astype(BF16) for s in st]
    outs = []
    for hh in range(MEM_HEADS):
        ot = jnp.dot(vmt_ref[0, hh * MEM_VT_ROWS:(hh + 1) * MEM_VT_ROWS], e[hh], preferred_element_type=F32)
        outs.append(ot[:MEM_HEAD_DIM] * (1.0 / ot[MEM_HEAD_DIM:MEM_HEAD_DIM + 1]))
    ymem_ref[...] = jnp.concatenate(outs, axis=0).T.astype(BF16)

    u = jax.nn.gelu(h_uv[:, :GM_WIDTH])
    v = jax.nn.gelu(h_uv[:, GM_WIDTH:])
    vn = _layer_norm(v, gmg_ref[...], gmb_ref[...]).astype(BF16)
    row = lax.broadcasted_iota(jnp.int32, (GM_CHUNK, GM_CHUNK), 0)
    col = lax.broadcasted_iota(jnp.int32, (GM_CHUNK, GM_CHUNK), 1)
    causal = col <= row
    for g in range(GM_GROUPS):
        wc = jnp.where(causal, ws_ref[g], 0.0).astype(BF16)
        bcol = bst_ref[:, g:g + 1]
        gs = slice(g * GM_GROUP_DIM, (g + 1) * GM_GROUP_DIM)
        for c in range(tm // GM_CHUNK):
            rs = slice(c * GM_CHUNK, (c + 1) * GM_CHUNK)
            mixed = jnp.dot(wc, vn[rs, gs], preferred_element_type=F32) + bcol
            ygm_ref[rs, gs] = (u[rs, gs] * mixed).astype(BF16)
    h_vd = project(OFF_VD, DIFF_V_WIDTH)

    ang = invf_ref[...] * pos_ref[0].astype(F32)
    cos_t = jnp.cos(ang)
    sin_t = jnp.sin(ang)
    rest = DIFF_HEAD_DIM - ROPE_DIM
    one_t = jnp.ones((rest, tm), F32)
    zero_t = jnp.zeros((rest, tm), F32)
    zero_h = jnp.zeros((ROPE_HALF, tm), F32)
    cos = jnp.concatenate([cos_t, cos_t, one_t] * 2, axis=0).T
    sin_lo = jnp.concatenate([-sin_t, zero_h, zero_t] * 2, axis=0).T
    sin_hi = jnp.concatenate([zero_h, sin_t, zero_t] * 2, axis=0).T

    def rope(t):
        return (t * cos + pltpu.roll(t, LANES - ROPE_HALF, axis=1) * sin_lo
                + pltpu.roll(t, ROPE_HALF, axis=1) * sin_hi)

    scale = DIFF_HEAD_DIM ** -0.5 * math.log2(math.e)
    for j in range(DIFF_Q_WIDTH // LANES):
        cs = slice(j * LANES, (j + 1) * LANES)
        q_ref[:, cs] = (rope(h_qk[:, j * LANES:(j + 1) * LANES]) * scale).astype(BF16)
        k_ref[:, cs] = rope(h_qk[:, DIFF_Q_WIDTH + j * LANES:DIFF_Q_WIDTH + (j + 1) * LANES]).astype(BF16)
    vt = h_vd.T.astype(BF16)
    ones = jnp.ones((VT_ROWS - DIFF_V_DIM, tm), BF16)
    for hh in range(DIFF_HEADS):
        vt_ref[0, hh * VT_ROWS:hh * VT_ROWS + DIFF_V_DIM] = vt[hh * DIFF_V_DIM:(hh + 1) * DIFF_V_DIM]
        vt_ref[0, hh * VT_ROWS + DIFF_V_DIM:(hh + 1) * VT_ROWS] = ones


def _proj(x2d, pos3d, invf, w, gm_g, gm_b, w_s, b_st, km, vmt, seq, n_mem):
    t = x2d.shape[0]
    tiles_per_batch = seq // TM_PROJ
    row_spec = lambda width: pl.BlockSpec((TM_PROJ, width), lambda i: (i, 0))
    return pl.pallas_call(
        _proj_body,
        grid=(t // TM_PROJ,),
        in_specs=[
            row_spec(D_MODEL),
            pl.BlockSpec((1, 1, TM_PROJ), lambda i: (i, 0, 0)),
            _const_spec((ROPE_HALF, 1)),
            _const_spec((D_MODEL, IN_WIDTH)),
            _const_spec((1, GM_WIDTH)),
            _const_spec((1, GM_WIDTH)),
            _const_spec((GM_GROUPS, GM_CHUNK, GM_CHUNK)),
            _const_spec((GM_CHUNK, GM_GROUPS)),
            pl.BlockSpec((1, MEM_HEADS, n_mem, MEM_WIDTH), lambda i: (i // tiles_per_batch, 0, 0, 0)),
            pl.BlockSpec((1, MEM_HEADS * MEM_VT_ROWS, n_mem), lambda i: (i // tiles_per_batch, 0, 0)),
        ],
        out_specs=[row_spec(DIFF_Q_WIDTH), row_spec(DIFF_Q_WIDTH),
                   pl.BlockSpec((1, DIFF_HEADS * VT_ROWS, TM_PROJ),
                                lambda i: (i // tiles_per_batch, 0, i % tiles_per_batch)),
                   row_spec(GM_WIDTH), row_spec(MEM_WIDTH)],
        out_shape=[jax.ShapeDtypeStruct((t, DIFF_Q_WIDTH), BF16),
                   jax.ShapeDtypeStruct((t, DIFF_Q_WIDTH), BF16),
                   jax.ShapeDtypeStruct((t // seq, DIFF_HEADS * VT_ROWS, seq), BF16),
                   jax.ShapeDtypeStruct((t, GM_WIDTH), BF16),
                   jax.ShapeDtypeStruct((t, MEM_WIDTH), BF16)],
        compiler_params=_params(1),
        name="proj",
    )(x2d, pos3d, invf, w, gm_g, gm_b, w_s, b_st, km, vmt)


def _diff_attn_body(lamv_ref, ngc_ref, q_ref, k_ref, vt_ref, o_ref, qs_ref, st_ref, e_ref, acc_ref):
    qi = pl.program_id(1)
    heads = range(DIFF_HEADS)
    lane = lax.broadcasted_iota(jnp.int32, (TQ, DIFF_V_DIM), 1)
    for h in heads:
        q = q_ref[0, :, h * DIFF_V_DIM:(h + 1) * DIFF_V_DIM]
        zero = jnp.zeros_like(q)
        qs_ref[h, :TQ] = jnp.where(lane < DIFF_HEAD_DIM, q, zero)
        qs_ref[h, TQ:] = jnp.where(lane >= DIFF_HEAD_DIM, q, zero)
    acc_ref[...] = jnp.zeros(acc_ref.shape, F32)
    e_ref[...] = jnp.zeros(e_ref.shape, BF16)

    def scores(h, ki):
        kb = k_ref[0, pl.ds(pl.multiple_of(ki * TQ, TQ), TQ), h * DIFF_V_DIM:(h + 1) * DIFF_V_DIM]
        st = lax.dot_general(kb, qs_ref[h], NT_DIMS, preferred_element_type=F32)
        st_ref[h] = st
        return jnp.max(st, axis=0, keepdims=True)

    def values(h, ki):
        vtb = vt_ref[0, h * VT_ROWS:(h + 1) * VT_ROWS, pl.ds(pl.multiple_of(ki * TQ, TQ), TQ)]
        return jnp.dot(vtb, e_ref[h], preferred_element_type=F32)

    def softmax(h, st, m, bm):
        m_new = jnp.maximum(m, bm)
        e_ref[h] = jnp.exp2(st - m_new).astype(BF16)
        return jnp.exp2(m - m_new), m_new

    def step(t, carry):
        a_prev, m, bm = carry
        pv = [values(h, jnp.maximum(t - 1, 0)) for h in heads]
        am =[softmax(h, st_ref[h], m[h], bm[h]) for h in heads]
        for h in heads:
            acc_ref[h] = a_prev[h] * acc_ref[h] + pv[h]
        return (tuple(x[0] for x in am), tuple(x[1] for x in am),
                tuple(scores(h, t + 1) for h in heads))

    init = (tuple(jnp.ones((1, 2 * TQ), F32) for _ in heads),
            tuple(jnp.full((1, 2 * TQ), -jnp.inf, F32) for _ in heads),
            tuple(scores(h, 0) for h in heads))
    a_prev, m, _ = lax.fori_loop(0, qi, step, init)

    lv = lamv_ref[...]
    lam = (jnp.exp(jnp.sum(lv[0:1] * lv[1:2], axis=-1, keepdims=True))
           - jnp.exp(jnp.sum(lv[2:3] * lv[3:4], axis=-1, keepdims=True)) + LAM_INIT)
    kpos = lax.broadcasted_iota(jnp.int32, (TQ, 2 * TQ), 0)
    qpos = lax.broadcasted_iota(jnp.int32, (TQ, 2 * TQ), 1) % TQ
    visible = kpos <= qpos
    pv = [values(h, jnp.maximum(qi - 1, 0)) for h in heads]
    for h in heads:
        st = jnp.where(visible, st_ref[h], -jnp.inf)
        a, _ = softmax(h, st, m[h], jnp.max(st, axis=0, keepdims=True))
        acc = a * (a_prev[h] * acc_ref[h] + pv[h]) + values(h, qi)
        on = acc[:DIFF_V_DIM] * (1.0 / acc[DIFF_V_DIM:DIFF_V_DIM + 1])
        ot = on[:, :TQ] - lam * on[:, TQ:]
        ot = ot * lax.rsqrt(jnp.mean(ot * ot, axis=0, keepdims=True) + LN_EPS) * ngc_ref[...]
        o_ref[0, :, h * DIFF_V_DIM:(h + 1) * DIFF_V_DIM] = (ot * (1.0 - LAM_INIT)).T.astype(BF16)


def _diff_attn(lamv, norm_g_col, q, k, vt):
    b, s, _ = q.shape
    return pl.pallas_call(
        _diff_attn_body,
        grid=(b, s // TQ),
        in_specs=[
            _const_spec((4, DIFF_HEAD_DIM)),
            _const_spec((DIFF_V_DIM, 1)),
            pl.BlockSpec((1, TQ, DIFF_Q_WIDTH), lambda bi, qi: (bi, qi, 0)),
            pl.BlockSpec((1, s, DIFF_Q_WIDTH), lambda bi, qi: (bi, 0, 0)),
            pl.BlockSpec((1, DIFF_HEADS * VT_ROWS, s), lambda bi, qi: (bi, 0, 0)),
        ],
        out_specs=pl.BlockSpec((1, TQ, DIFF_V_WIDTH), lambda bi, qi: (bi, qi, 0)),
        out_shape=jax.ShapeDtypeStruct((b, s, DIFF_V_WIDTH), BF16),
        scratch_shapes=[pltpu.VMEM((DIFF_HEADS, 2 * TQ, DIFF_V_DIM), BF16),
                        pltpu.VMEM((DIFF_HEADS, TQ, 2 * TQ), F32),
                        pltpu.VMEM((DIFF_HEADS, TQ, 2 * TQ), BF16),
                        pltpu.VMEM((DIFF_HEADS, VT_ROWS, 2 * TQ), F32)],
        compiler_params=_params(2),
        name="diff_attn",
    )(lamv, norm_g_col, q, k, vt)


def _merge_body(x_ref, ygm_ref, ydf_ref, ymm_ref, wgate_ref, gb_ref, wgm_ref, wdf_ref, wmm_ref, wo_ref,
                g_ref, b_ref, o_ref, z_ref, m_ref):
    def tile(zero_rows):
        x = x_ref[...]
        xb = x.astype(BF16)
        branch_refs = ((ygm_ref, wgm_ref), (ydf_ref, wdf_ref), (ymm_ref, wmm_ref))
        for c in range(D_MODEL // MXU_COLS):
            cs = slice(c * MXU_COLS, (c + 1) * MXU_COLS)
            merged = zero_rows[c]
            for r, (y_ref, w_ref) in enumerate(branch_refs):
                gs = slice(r * D_MODEL + c * MXU_COLS, r * D_MODEL + (c + 1) * MXU_COLS)
                y = jnp.dot(y_ref[...], w_ref[:, cs], preferred_element_type=F32)
                logits = jnp.dot(xb, wgate_ref[:, OFF_GATE + gs.start:OFF_GATE + gs.stop],
                                 preferred_element_type=F32)
                merged = merged + (y + jnp.tanh(logits + gb_ref[:, gs]) * y)
            m_ref[:, cs] = merged.astype(BF16)
        y = jnp.dot(m_ref[...], wo_ref[...], preferred_element_type=F32)
        return DEEPNORM_ALPHA * x + y

    _run_lagged(tile, z_ref, g_ref, b_ref, o_ref)


def _merge(x2d, ygm, ydf, ymm, w_gate, gate_b, w_gm, w_df, w_mm, w_o, g, b):
    t = x2d.shape[0]
    n_tiles = t // TM_MERGE
    row_spec, out_spec = _lagged_row_specs(n_tiles, TM_MERGE)
    return pl.pallas_call(
        _merge_body,
        grid=(n_tiles + 1,),
        in_specs=[
            row_spec(D_MODEL), row_spec(GM_WIDTH), row_spec(DIFF_V_WIDTH), row_spec(MEM_WIDTH),
            _const_spec((D_MODEL, IN_WIDTH)),
            _const_spec((1, N_BRANCH * D_MODEL)),
            _const_spec((GM_WIDTH, D_MODEL)),
            _const_spec((DIFF_V_WIDTH, D_MODEL)),
            _const_spec((MEM_WIDTH, D_MODEL)),
            _const_spec((D_MODEL, D_MODEL)),
            _const_spec((1, D_MODEL)),
            _const_spec((1, D_MODEL)),
        ],
        out_specs=out_spec(D_MODEL),
        out_shape=jax.ShapeDtypeStruct((t, D_MODEL), F32),
        scratch_shapes=[pltpu.VMEM((TM_MERGE, D_MODEL), F32), pltpu.VMEM((TM_MERGE, D_MODEL), BF16)],
        compiler_params=_params_sequential(),
        name="merge",
    )(x2d, ygm, ydf, ymm, w_gate, gate_b, w_gm, w_df, w_mm, w_o, g, b)


def _rope_freqs():
    return (ROPE_THETA ** (-jnp.arange(0, ROPE_DIM, 2, dtype=F32) / ROPE_DIM)).reshape(ROPE_HALF, 1)


def kernel(x, mem, positions, ffn1_w_in, ffn1_w_out, ln1_g, ln1_b, w_in, gate_b, gm_ln_g, gm_ln_b, gm_w_s, gm_b_s,
           lambda_q1, lambda_k1, lambda_q2, lambda_k2, diff_norm_g, w_mem_kv, w_branch_gm, w_branch_diff,
           w_branch_mem, w_o, ln2_g, ln2_b, ffn2_w_in, ffn2_w_out, ln3_g, ln3_b):
    bsz, seq, _ = x.shape
    n_mem = mem.shape[1]
    t = bsz * seq
    i = 0
    x2d = x.reshape(t, D_MODEL)

    ffn_scale = jnp.concatenate([jnp.full((1, D_FF), 0.5, F32), jnp.ones((1, D_FF), F32)], axis=1)
    in_scale = jnp.concatenate([jnp.ones((1, OFF_GATE), F32), jnp.full((1, N_BRANCH * D_MODEL), 0.5, F32)], axis=1)

    x1, w_in_b, ffn2_w_in_b, ffn2_w_out_b = _ffn_ln(
        x2d, (ffn1_w_in[i] * ffn_scale).astype(BF16), ffn1_w_out[i].astype(BF16), ln1_g[i][None], ln1_b[i][None],
        casts=((w_in[i:i + 1], in_scale), (ffn2_w_in[i:i + 1], ffn_scale),
               (ffn2_w_out[i:i + 1], jnp.ones((1, D_MODEL), F32))))

    km, vmt = _mem_kv(mem.reshape(bsz * n_mem, D_MODEL), w_mem_kv[i].astype(BF16), n_mem)
    q, k, vt, ygm, ymm = _proj(
        x1, positions.reshape(t // TM_PROJ, 1, TM_PROJ), _rope_freqs(), w_in_b,
        gm_ln_g[i][None], gm_ln_b[i][None], gm_w_s[i], gm_b_s[i].T, km, vmt, seq, n_mem)

    lamv = jnp.stack([lambda_q1[i], lambda_k1[i], lambda_q2[i], lambda_k2[i]]).astype(F32)
    ydf = _diff_attn(lamv, diff_norm_g[i][:, None],
                     q.reshape(bsz, seq, DIFF_Q_WIDTH), k.reshape(bsz, seq, DIFF_Q_WIDTH),
                     vt).reshape(t, DIFF_V_WIDTH)

    x2 = _merge(x1, ygm, ydf, ymm, w_in_b, 0.5 * gate_b[i][None],
                w_branch_gm[i].astype(BF16), w_branch_diff[i].astype(BF16), w_branch_mem[i].astype(BF16),
                (0.5 * w_o[i]).astype(BF16), ln2_g[i][None], ln2_b[i][None])

    x3, = _ffn_ln(x2, ffn2_w_in_b, ffn2_w_out_b, ln3_g[i][None], ln3_b[i][None])
    return x3.reshape(bsz, seq, D_MODEL)
```

```python
import functools
import math

import jax
import jax.numpy as jnp
from jax import lax
from jax.experimental import pallas as pl
from jax.experimental.pallas import tpu as pltpu

F32 = jnp.float32
BF16 = jnp.bfloat16

D_MODEL = 1024
D_FF = 2816
GM_WIDTH = 512
GM_GROUPS = 4
GM_GROUP_DIM = GM_WIDTH // GM_GROUPS
GM_CHUNK = 128
DIFF_HEADS = 4
DIFF_HEAD_DIM = 64
DIFF_V_DIM = 2 * DIFF_HEAD_DIM
DIFF_Q_WIDTH = DIFF_HEADS * 2 * DIFF_HEAD_DIM
DIFF_V_WIDTH = DIFF_HEADS * DIFF_V_DIM
MEM_HEADS = 4
MEM_HEAD_DIM = 64
MEM_WIDTH = MEM_HEADS * MEM_HEAD_DIM
N_BRANCH = 3
ROPE_THETA = 500000.0
ROPE_DIM = DIFF_HEAD_DIM // 4
ROPE_HALF = ROPE_DIM // 2
DEPTH = 1
DEEPNORM_ALPHA = (2 * DEPTH) ** 0.25
LN_EPS = 1e-5
LAM_INIT = 0.8 - 0.6 * math.exp(-0.3 * 0)
OFF_U = 0
OFF_V = OFF_U + GM_WIDTH
OFF_Q = OFF_V + GM_WIDTH
OFF_K = OFF_Q + DIFF_Q_WIDTH
OFF_VD = OFF_K + DIFF_Q_WIDTH
OFF_QM = OFF_VD + DIFF_V_WIDTH
OFF_GATE = OFF_QM + MEM_WIDTH
IN_WIDTH = OFF_GATE + N_BRANCH * D_MODEL

LANES = 128
BF16_SUBLANES = 16
VT_ROWS = DIFF_V_DIM + BF16_SUBLANES
MEM_VT_ROWS = MEM_HEAD_DIM + BF16_SUBLANES
MXU_COLS = 256
VMEM_LIMIT = 52 * 1024 * 1024

TM_FFN = 1024
TM_PROJ = 512
TM_MERGE = 1024
TQ = 512

NT_DIMS = (((1,), (1,)), ((), ()))


def _const_spec(shape):
    nd = len(shape)
    return pl.BlockSpec(shape, lambda *_: (0,) * nd, pipeline_mode=pl.Buffered(1))


def _params(n_axes):
    return pltpu.CompilerParams(dimension_semantics=("parallel",) * n_axes,
                                vmem_limit_bytes=VMEM_LIMIT)


def _params_sequential():
    return pltpu.CompilerParams(dimension_semantics=("arbitrary",), vmem_limit_bytes=VMEM_LIMIT)


def _layer_norm(z, g, b):
    mu = jnp.mean(z, axis=-1, keepdims=True)
    zc = z - mu
    var = jnp.mean(zc * zc, axis=-1, keepdims=True)
    return zc * lax.rsqrt(var + LN_EPS) * g + b


def _lagged_row_specs(n_tiles, tm):
    in_spec = lambda width: pl.BlockSpec((tm, width), lambda i: (jnp.minimum(i, n_tiles - 1), 0))
    out_spec = lambda width: pl.BlockSpec((tm, width), lambda i: (jnp.maximum(i - 1, 0), 0))
    return in_spec, out_spec


def _lagged_norm(z_ref, g_ref, b_ref, o_ref):
    n_groups = z_ref.shape[1] // MXU_COLS
    rows = z_ref.shape[0] // n_groups
    zero_rows = []
    for j in range(n_groups):
        rs = slice(j * rows, (j + 1) * rows)
        out = _layer_norm(z_ref[rs, :], g_ref[...], b_ref[...])
        o_ref[rs, :] = out
        bits = lax.bitcast_convert_type(jnp.max(out, axis=0, keepdims=True)[:, :MXU_COLS], jnp.uint32)
        zero_rows.append(lax.bitcast_convert_type((bits >> 16) >> 16, F32))
    return zero_rows


def _run_lagged(tile_fn, z_ref, g_ref, b_ref, o_ref):
    step = pl.program_id(0)
    last = pl.num_programs(0) - 1

    @pl.when(step == 0)
    def _():
        z_ref[...] = jnp.zeros(z_ref.shape, F32)

    @pl.when(step < last)
    def _():
        zero_rows = _lagged_norm(z_ref, g_ref, b_ref, o_ref)
        z_ref[...] = tile_fn(zero_rows)

    @pl.when(step == last)
    def _():
        _lagged_norm(z_ref, g_ref, b_ref, o_ref)


def _ffn_ln_body(n_casts, x_ref, win_ref, wout_ref, g_ref, b_ref, *refs):
    cast_in, o_ref, cast_out = refs[:2 * n_casts], refs[2 * n_casts], refs[2 * n_casts + 1:3 * n_casts + 1]
    z_ref, h_ref = refs[3 * n_casts + 1:]

    def tile(zero_rows):
        for j in range(n_casts):
            cast_out[j][...] = (cast_in[2 * j][0] * cast_in[2 * j + 1][...]).astype(BF16)
        x = x_ref[...]
        xb = x.astype(BF16)
        for c in range(D_FF // MXU_COLS):
            cs = slice(c * MXU_COLS, (c + 1) * MXU_COLS)
            g = jnp.dot(xb, win_ref[:, cs], preferred_element_type=F32)
            u = jnp.dot(xb, win_ref[:, D_FF + c * MXU_COLS:D_FF + (c + 1) * MXU_COLS], preferred_element_type=F32)
            if c < len(zero_rows):
                g = g + zero_rows[c]
            h_ref[:, cs] = ((g + g * jnp.tanh(g)) * u).astype(BF16)
        y = jnp.dot(h_ref[...], wout_ref[...], preferred_element_type=F32)
        return DEEPNORM_ALPHA * x + 0.5 * y

    _run_lagged(tile, z_ref, g_ref, b_ref, o_ref)


def _ffn_ln(x2d, w_in, w_out, g, b, casts=()):
    t = x2d.shape[0]
    n_tiles = t // TM_FFN
    in_spec, out_spec = _lagged_row_specs(n_tiles, TM_FFN)
    cast_specs, cast_args, cast_out_specs, cast_out_shapes = [], [], [], []
    for src, scale in casts:
        _, rows, cols = src.shape
        blk = next(r for r in range(BF16_SUBLANES, rows + 1, BF16_SUBLANES)
                   if rows % r == 0 and rows // r <= n_tiles)
        last = rows // blk - 1
        cast_specs += [pl.BlockSpec((1, blk, cols), lambda i, last=last: (0, jnp.minimum(i, last), 0)),
                       _const_spec((1, cols))]
        cast_args += [src, scale]
        cast_out_specs.append(pl.BlockSpec((blk, cols), lambda i, last=last: (jnp.minimum(i, last), 0)))
        cast_out_shapes.append(jax.ShapeDtypeStruct((rows, cols), BF16))
    return pl.pallas_call(
        functools.partial(_ffn_ln_body, len(casts)),
        grid=(n_tiles + 1,),
        in_specs=[
            in_spec(D_MODEL),
            _const_spec((D_MODEL, 2 * D_FF)),
            _const_spec((D_FF, D_MODEL)),
            _const_spec((1, D_MODEL)),
            _const_spec((1, D_MODEL)),
        ] + cast_specs,
        out_specs=[out_spec(D_MODEL)] + cast_out_specs,
        out_shape=[jax.ShapeDtypeStruct((t, D_MODEL), F32)] + cast_out_shapes,
        scratch_shapes=[pltpu.VMEM((TM_FFN, D_MODEL), F32), pltpu.VMEM((TM_FFN, D_FF), BF16)],
        compiler_params=_params_sequential(),
        name="ffn_ln",
    )(x2d, w_in, w_out, g, b, *cast_args)


def _mem_kv_body(m_ref, w_ref, km_ref, vmt_ref):
    n_mem = m_ref.shape[0]
    kv = jnp.dot(m_ref[...].astype(BF16), w_ref[...], preferred_element_type=F32)
    k = kv[:, :MEM_WIDTH].astype(BF16)
    vt = kv[:, MEM_WIDTH:].T.astype(BF16)
    lane = lax.broadcasted_iota(jnp.int32, k.shape, 1)
    ones = jnp.ones((MEM_VT_ROWS - MEM_HEAD_DIM, n_mem), BF16)
    for hh in range(MEM_HEADS):
        km_ref[0, hh] = jnp.where(lane // MEM_HEAD_DIM == hh, k, jnp.zeros_like(k))
        vmt_ref[0, hh * MEM_VT_ROWS:hh * MEM_VT_ROWS + MEM_HEAD_DIM] = vt[hh * MEM_HEAD_DIM:(hh + 1) * MEM_HEAD_DIM]
        vmt_ref[0, hh * MEM_VT_ROWS + MEM_HEAD_DIM:(hh + 1) * MEM_VT_ROWS] = ones


def _mem_kv(mem2d, w_kv, n_mem):
    bsz = mem2d.shape[0] // n_mem
    return pl.pallas_call(
        _mem_kv_body,
        grid=(bsz,),
        in_specs=[
            pl.BlockSpec((n_mem, D_MODEL), lambda i: (i, 0)),
            _const_spec((D_MODEL, 2 * MEM_WIDTH)),
        ],
        out_specs=[pl.BlockSpec((1, MEM_HEADS, n_mem, MEM_WIDTH), lambda i: (i, 0, 0, 0)),
                   pl.BlockSpec((1, MEM_HEADS * MEM_VT_ROWS, n_mem), lambda i: (i, 0, 0))],
        out_shape=[jax.ShapeDtypeStruct((bsz, MEM_HEADS, n_mem, MEM_WIDTH), BF16),
                   jax.ShapeDtypeStruct((bsz, MEM_HEADS * MEM_VT_ROWS, n_mem), BF16)],
        compiler_params=_params(1),
        name="mem_kv",
    )(mem2d, w_kv)


def _proj_body(x_ref, pos_ref, invf_ref, w_ref, gmg_ref, gmb_ref, ws_ref, bst_ref, km_ref, vmt_ref,
               q_ref, k_ref, vt_ref, ygm_ref, ymem_ref):
    tm = x_ref.shape[0]
    xb = x_ref[...].astype(BF16)

    def project(off, width):
        return jnp.dot(xb, w_ref[:, off:off + width], preferred_element_type=F32)

    h_qm = project(OFF_QM, MEM_WIDTH)
    h_uv = project(OFF_U, 2 * GM_WIDTH)

    qm = (h_qm * (MEM_HEAD_DIM ** -0.5 * math.log2(math.e))).astype(BF16)
    st = [lax.dot_general(km_ref[0, hh], qm, NT_DIMS, preferred_element_type=F32) for hh in range(MEM_HEADS)]
    h_qk = project(OFF_Q, 2 * DIFF_Q_WIDTH)
    e = [jnp.exp2(s - jnp.max(s, axis=0, keepdims=True)).astype(BF16) for s in st]
    outs = []
    for hh in range(MEM_HEADS):
        ot = jnp.dot(vmt_ref[0, hh * MEM_VT_ROWS:(hh + 1) * MEM_VT_ROWS], e[hh], preferred_element_type=F32)
        outs.append(ot[:MEM_HEAD_DIM] * (1.0 / ot[MEM_HEAD_DIM:MEM_HEAD_DIM + 1]))
    ymem_ref[...] = jnp.concatenate(outs, axis=0).T.astype(BF16)

    u = jax.nn.gelu(h_uv[:, :GM_WIDTH])
    v = jax.nn.gelu(h_uv[:, GM_WIDTH:])
    vn = _layer_norm(v, gmg_ref[...], gmb_ref[...]).astype(BF16)
    row = lax.broadcasted_iota(jnp.int32, (GM_CHUNK, GM_CHUNK), 0)
    col = lax.broadcasted_iota(jnp.int32, (GM_CHUNK, GM_CHUNK), 1)
    causal = col <= row
    for g in range(GM_GROUPS):
        wc = jnp.where(causal, ws_ref[g], 0.0).astype(BF16)
        bcol = bst_ref[:, g:g + 1]
        gs = slice(g * GM_GROUP_DIM, (g + 1) * GM_GROUP_DIM)
        for c in range(tm // GM_CHUNK):
            rs = slice(c * GM_CHUNK, (c + 1) * GM_CHUNK)
            mixed = jnp.dot(wc, vn[rs, gs], preferred_element_type=F32) + bcol
            ygm_ref[rs, gs] = (u[rs, gs] * mixed).astype(BF16)
    h_vd = project(OFF_VD, DIFF_V_WIDTH)

    ang = invf_ref[...] * pos_ref[0].astype(F32)
    cos_t = jnp.cos(ang)
    sin_t = jnp.sin(ang)
    rest = DIFF_HEAD_DIM - ROPE_DIM
    one_t = jnp.ones((rest, tm), F32)
    zero_t = jnp.zeros((rest, tm), F32)
    zero_h = jnp.zeros((ROPE_HALF, tm), F32)
    cos = jnp.concatenate([cos_t, cos_t, one_t] * 2, axis=0).T
    sin_lo = jnp.concatenate([-sin_t, zero_h, zero_t] * 2, axis=0).T
    sin_hi = jnp.concatenate([zero_h, sin_t, zero_t] * 2, axis=0).T

    def rope(t):
        return (t * cos + pltpu.roll(t, LANES - ROPE_HALF, axis=1) * sin_lo
                + pltpu.roll(t, ROPE_HALF, axis=1) * sin_hi)

    scale = DIFF_HEAD_DIM ** -0.5 * math.log2(math.e)
    for j in range(DIFF_Q_WIDTH // LANES):
        cs = slice(j * LANES, (j + 1) * LANES)
        q_ref[:, cs] = (rope(h_qk[:, j * LANES:(j + 1) * LANES]) * scale).astype(BF16)
        k_ref[:, cs] = rope(h_qk[:, DIFF_Q_WIDTH + j * LANES:DIFF_Q_WIDTH + (j + 1) * LANES]).astype(BF16)
    vt = h_vd.T.astype(BF16)
    ones = jnp.ones((VT_ROWS - DIFF_V_DIM, tm), BF16)
    for hh in range(DIFF_HEADS):
        vt_ref[0, hh * VT_ROWS:hh * VT_ROWS + DIFF_V_DIM] = vt[hh * DIFF_V_DIM:(hh + 1) * DIFF_V_DIM]
        vt_ref[0, hh * VT_ROWS + DIFF_V_DIM:(hh + 1) * VT_ROWS] = ones


def _proj(x2d, pos3d, invf, w, gm_g, gm_b, w_s, b_st, km, vmt, seq, n_mem):
    t = x2d.shape[0]
    tiles_per_batch = seq // TM_PROJ
    row_spec = lambda width: pl.BlockSpec((TM_PROJ, width), lambda i: (i, 0))
    return pl.pallas_call(
        _proj_body,
        grid=(t // TM_PROJ,),
        in_specs=[
            row_spec(D_MODEL),
            pl.BlockSpec((1, 1, TM_PROJ), lambda i: (i, 0, 0)),
            _const_spec((ROPE_HALF, 1)),
            _const_spec((D_MODEL, IN_WIDTH)),
            _const_spec((1, GM_WIDTH)),
            _const_spec((1, GM_WIDTH)),
            _const_spec((GM_GROUPS, GM_CHUNK, GM_CHUNK)),
            _const_spec((GM_CHUNK, GM_GROUPS)),
            pl.BlockSpec((1, MEM_HEADS, n_mem, MEM_WIDTH), lambda i: (i // tiles_per_batch, 0, 0, 0)),
            pl.BlockSpec((1, MEM_HEADS * MEM_VT_ROWS, n_mem), lambda i: (i // tiles_per_batch, 0, 0)),
        ],
        out_specs=[row_spec(DIFF_Q_WIDTH), row_spec(DIFF_Q_WIDTH),
                   pl.BlockSpec((1, DIFF_HEADS * VT_ROWS, TM_PROJ),
                                lambda i: (i // tiles_per_batch, 0, i % tiles_per_batch)),
                   row_spec(GM_WIDTH), row_spec(MEM_WIDTH)],
        out_shape=[jax.ShapeDtypeStruct((t, DIFF_Q_WIDTH), BF16),
                   jax.ShapeDtypeStruct((t, DIFF_Q_WIDTH), BF16),
                   jax.ShapeDtypeStruct((t // seq, DIFF_HEADS * VT_ROWS, seq), BF16),
                   jax.ShapeDtypeStruct((t, GM_WIDTH), BF16),
                   jax.ShapeDtypeStruct((t, MEM_WIDTH), BF16)],
        compiler_params=_params(1),
        name="proj",
    )(x2d, pos3d, invf, w, gm_g, gm_b, w_s, b_st, km, vmt)


def _diff_attn_body(lamv_ref, ngc_ref, q_ref, k_ref, vt_ref, o_ref, qs_ref, st_ref, e_ref, acc_ref):
    qi = pl.program_id(1)
    heads = range(DIFF_HEADS)
    lane = lax.broadcasted_iota(jnp.int32, (TQ, DIFF_V_DIM), 1)
    for h in heads:
        q = q_ref[0, :, h * DIFF_V_DIM:(h + 1) * DIFF_V_DIM]
        zero = jnp.zeros_like(q)
        qs_ref[h, :TQ] = jnp.where(lane < DIFF_HEAD_DIM, q, zero)
        qs_ref[h, TQ:] = jnp.where(lane >= DIFF_HEAD_DIM, q, zero)

    def scores(h, ki, clear=False):
        kb = k_ref[0, pl.ds(pl.multiple_of(ki * TQ, TQ), TQ), h * DIFF_V_DIM:(h + 1) * DIFF_V_DIM]
        st = lax.dot_general(kb, qs_ref[h], NT_DIMS, preferred_element_type=F32)
        st_ref[h] = st
        if clear:
            zero = lax.bitcast_convert_type((lax.bitcast_convert_type(st, jnp.uint32) >> 16) >> 16, F32)
            e_ref[h] = zero.astype(BF16)
            acc_ref[h] = zero[:VT_ROWS]
        return jnp.max(st, axis=0, keepdims=True)

    def values(h, ki):
        vtb = vt_ref[0, h * VT_ROWS:(h + 1) * VT_ROWS, pl.ds(pl.multiple_of(ki * TQ, TQ), TQ)]
        return jnp.dot(vtb, e_ref[h], preferred_element_type=F32)

    def softmax(h, st, m, bm):
        m_new = jnp.maximum(m, bm)
        e_ref[h] = jnp.exp2(st - m_new).astype(BF16)
        return jnp.exp2(m - m_new), m_new

    def step(t, carry):
        a_prev, m, bm = carry
        pv = [values(h, jnp.maximum(t - 1, 0)) for h in heads]
        am = [softmax(h, st_ref[h], m[h], bm[h]) for h in heads]
        for h in heads:
            acc_ref[h] = a_prev[h] * acc_ref[h] + pv[h]
        return (tuple(x[0] for x in am), tuple(x[1] for x in am),
                tuple(scores(h, t + 1) for h in heads))

    init = (tuple(jnp.ones((1, 2 * TQ), F32) for _ in heads),
            tuple(jnp.full((1, 2 * TQ), -jnp.inf, F32) for _ in heads),
            tuple(scores(h, 0, clear=True) for h in heads))
    a_prev, m, _ = lax.fori_loop(0, qi, step, init)

    lv = lamv_ref[...]
    lam = (jnp.exp(jnp.sum(lv[0:1] * lv[1:2], axis=-1, keepdims=True))
           - jnp.exp(jnp.sum(lv[2:3] * lv[3:4], axis=-1, keepdims=True)) + LAM_INIT)
    kpos = lax.broadcasted_iota(jnp.int32, (TQ, 2 * TQ), 0)
    qpos = lax.broadcasted_iota(jnp.int32, (TQ, 2 * TQ), 1) % TQ
    visible = kpos <= qpos
    pv = [values(h, jnp.maximum(qi - 1, 0)) for h in heads]
    for h in heads:
        st = jnp.where(visible, st_ref[h], -jnp.inf)
        a, _ = softmax(h, st, m[h], jnp.max(st, axis=0, keepdims=True))
        acc = a * (a_prev[h] * acc_ref[h] + pv[h]) + values(h, qi)
        on = acc[:DIFF_V_DIM] * (1.0 / acc[DIFF_V_DIM:DIFF_V_DIM + 1])
        ot = on[:, :TQ] - lam * on[:, TQ:]
        ot = ot * lax.rsqrt(jnp.mean(ot * ot, axis=0, keepdims=True) + LN_EPS) * ngc_ref[...]
        o_ref[0, :, h * DIFF_V_DIM:(h + 1) * DIFF_V_DIM] = (ot * (1.0 - LAM_INIT)).T.astype(BF16)


def _diff_attn(lamv, norm_g_col, q, k, vt):
    b, s, _ = q.shape
    return pl.pallas_call(
        _diff_attn_body,
        grid=(b, s // TQ),
        in_specs=[
            _const_spec((4, DIFF_HEAD_DIM)),
            _const_spec((DIFF_V_DIM, 1)),
            pl.BlockSpec((1, TQ, DIFF_Q_WIDTH), lambda bi, qi: (bi, qi, 0)),
            pl.BlockSpec((1, s, DIFF_Q_WIDTH), lambda bi, qi: (bi, 0, 0)),
            pl.BlockSpec((1, DIFF_HEADS * VT_ROWS, s), lambda bi, qi: (bi, 0, 0)),
        ],
        out_specs=pl.BlockSpec((1, TQ, DIFF_V_WIDTH), lambda bi, qi: (bi, qi, 0)),
        out_shape=jax.ShapeDtypeStruct((b, s, DIFF_V_WIDTH), BF16),
        scratch_shapes=[pltpu.VMEM((DIFF_HEADS, 2 * TQ, DIFF_V_DIM), BF16),
                        pltpu.VMEM((DIFF_HEADS, TQ, 2 * TQ), F32),
                        pltpu.VMEM((DIFF_HEADS, TQ, 2 * TQ), BF16),
                        pltpu.VMEM((DIFF_HEADS, VT_ROWS, 2 * TQ), F32)],
        compiler_params=_params(2),
        name="diff_attn",
    )(lamv, norm_g_col, q, k, vt)


def _merge_body(x_ref, ygm_ref, ydf_ref, ymm_ref, wgate_ref, gb_ref, wgm_ref, wdf_ref, wmm_ref, wo_ref,
                g_ref, b_ref, o_ref, z_ref, m_ref):
    def tile(zero_rows):
        x = x_ref[...]
        xb = x.astype(BF16)
        branch_refs = ((ygm_ref, wgm_ref), (ydf_ref, wdf_ref), (ymm_ref, wmm_ref))
        for c in range(D_MODEL // MXU_COLS):
            cs = slice(c * MXU_COLS, (c + 1) * MXU_COLS)
            merged = zero_rows[c]
            for r, (y_ref, w_ref) in enumerate(branch_refs):
                gs = slice(r * D_MODEL + c * MXU_COLS, r * D_MODEL + (c + 1) * MXU_COLS)
                y = jnp.dot(y_ref[...], w_ref[:, cs], preferred_element_type=F32)
                logits = jnp.dot(xb, wgate_ref[:, OFF_GATE + gs.start:OFF_GATE + gs.stop],
                                 preferred_element_type=F32)
                merged = merged + (y + jnp.tanh(logits + gb_ref[:, gs]) * y)
            m_ref[:, cs] = merged.astype(BF16)
        y = jnp.dot(m_ref[...], wo_ref[...], preferred_element_type=F32)
        return DEEPNORM_ALPHA * x + y

    _run_lagged(tile, z_ref, g_ref, b_ref, o_ref)


def _merge(x2d, ygm, ydf, ymm, w_gate, gate_b, w_gm, w_df, w_mm, w_o, g, b):
    t = x2d.shape[0]
    n_tiles = t // TM_MERGE
    row_spec, out_spec = _lagged_row_specs(n_tiles, TM_MERGE)
    return pl.pallas_call(
        _merge_body,
        grid=(n_tiles + 1,),
        in_specs=[
            row_spec(D_MODEL), row_spec(GM_WIDTH), row_spec(DIFF_V_WIDTH), row_spec(MEM_WIDTH),
            _const_spec((D_MODEL, IN_WIDTH)),
            _const_spec((1, N_BRANCH * D_MODEL)),
            _const_spec((GM_WIDTH, D_MODEL)),
            _const_spec((DIFF_V_WIDTH, D_MODEL)),
            _const_spec((MEM_WIDTH, D_MODEL)),
            _const_spec((D_MODEL, D_MODEL)),
            _const_spec((1, D_MODEL)),
            _const_spec((1, D_MODEL)),
        ],
        out_specs=out_spec(D_MODEL),
        out_shape=jax.ShapeDtypeStruct((t, D_MODEL), F32),
        scratch_shapes=[pltpu.VMEM((TM_MERGE, D_MODEL), F32), pltpu.VMEM((TM_MERGE, D_MODEL), BF16)],
        compiler_params=_params_sequential(),
        name="merge",
    )(x2d, ygm, ydf, ymm, w_gate, gate_b, w_gm, w_df, w_mm, w_o, g, b)


def _rope_freqs():
    return (ROPE_THETA ** (-jnp.arange(0, ROPE_DIM, 2, dtype=F32) / ROPE_DIM)).reshape(ROPE_HALF, 1)


def kernel(x, mem, positions, ffn1_w_in, ffn1_w_out, ln1_g, ln1_b, w_in, gate_b, gm_ln_g, gm_ln_b, gm_w_s, gm_b_s,
           lambda_q1, lambda_k1, lambda_q2, lambda_k2, diff_norm_g, w_mem_kv, w_branch_gm, w_branch_diff,
           w_branch_mem, w_o, ln2_g, ln2_b, ffn2_w_in, ffn2_w_out, ln3_g, ln3_b):
    bsz, seq, _ = x.shape
    n_mem = mem.shape[1]
    t = bsz * seq
    i = 0
    x2d = x.reshape(t, D_MODEL)

    ffn_scale = jnp.concatenate([jnp.full((1, D_FF), 0.5, F32), jnp.ones((1, D_FF), F32)], axis=1)
    in_scale = jnp.concatenate([jnp.ones((1, OFF_GATE), F32), jnp.full((1, N_BRANCH * D_MODEL), 0.5, F32)], axis=1)

    x1, w_in_b, ffn2_w_in_b, ffn2_w_out_b = _ffn_ln(
        x2d, (ffn1_w_in[i] * ffn_scale).astype(BF16), ffn1_w_out[i].astype(BF16), ln1_g[i][None], ln1_b[i][None],
        casts=((w_in[i:i + 1], in_scale), (ffn2_w_in[i:i + 1], ffn_scale),
               (ffn2_w_out[i:i + 1], jnp.ones((1, D_MODEL), F32))))

    km, vmt = _mem_kv(mem.reshape(bsz * n_mem, D_MODEL), w_mem_kv[i].astype(BF16), n_mem)
    q, k, vt, ygm, ymm = _proj(
        x1, positions.reshape(t // TM_PROJ, 1, TM_PROJ), _rope_freqs(), w_in_b,
        gm_ln_g[i][None], gm_ln_b[i][None], gm_w_s[i], gm_b_s[i].T, km, vmt, seq, n_mem)

    lamv = jnp.stack([lambda_q1[i], lambda_k1[i], lambda_q2[i], lambda_k2[i]]).astype(F32)
    ydf = _diff_attn(lamv, diff_norm_g[i][:, None],
                     q.reshape(bsz, seq, DIFF_Q_WIDTH), k.reshape(bsz, seq, DIFF_Q_WIDTH),
                     vt).reshape(t, DIFF_V_WIDTH)

    x2 = _merge(x1, ygm, ydf, ymm, w_in_b, 0.5 * gate_b[i][None],
                w_branch_gm[i].astype(BF16), w_branch_diff[i].astype(BF16), w_branch_mem[i].astype(BF16),
                (0.5 * w_o[i]).astype(BF16), ln2_g[i][None], ln2_b[i][None])

    x3, = _ffn_ln(x2, ffn2_w_in_b, ffn2_w_out_b, ln3_g[i][None], ln3_b[i][None])
    return x3.reshape(bsz, seq, D_MODEL)
```

```python
import functools
import math

import jax
import jax.numpy as jnp
from jax import lax
from jax.experimental import pallas as pl
from jax.experimental.pallas import tpu as pltpu

F32 = jnp.float32
BF16 = jnp.bfloat16

D_MODEL = 1024
D_FF = 2816
GM_WIDTH = 512
GM_GROUPS = 4
GM_GROUP_DIM = GM_WIDTH // GM_GROUPS
GM_CHUNK = 128
DIFF_HEADS = 4
DIFF_HEAD_DIM = 64
DIFF_V_DIM = 2 * DIFF_HEAD_DIM
DIFF_Q_WIDTH = DIFF_HEADS * 2 * DIFF_HEAD_DIM
DIFF_V_WIDTH = DIFF_HEADS * DIFF_V_DIM
MEM_HEADS = 4
MEM_HEAD_DIM = 64
MEM_WIDTH = MEM_HEADS * MEM_HEAD_DIM
N_BRANCH = 3
ROPE_THETA = 500000.0
ROPE_DIM = DIFF_HEAD_DIM // 4
ROPE_HALF = ROPE_DIM // 2
DEPTH = 1
DEEPNORM_ALPHA = (2 * DEPTH) ** 0.25
LN_EPS = 1e-5
LAM_INIT = 0.8 - 0.6 * math.exp(-0.3 * 0)
OFF_U = 0
OFF_V = OFF_U + GM_WIDTH
OFF_Q = OFF_V + GM_WIDTH
OFF_K = OFF_Q + DIFF_Q_WIDTH
OFF_VD = OFF_K + DIFF_Q_WIDTH
OFF_QM = OFF_VD + DIFF_V_WIDTH
OFF_GATE = OFF_QM + MEM_WIDTH
IN_WIDTH = OFF_GATE + N_BRANCH * D_MODEL

LANES = 128
BF16_SUBLANES = 16
VT_ROWS = DIFF_V_DIM + BF16_SUBLANES
MEM_VT_ROWS = MEM_HEAD_DIM + BF16_SUBLANES
MXU_COLS = 256
VMEM_LIMIT = 52 * 1024 * 1024

TM_FFN = 1024
TM_PROJ = 512
TM_MERGE = 1024
TQ = 512

NT_DIMS = (((1,), (1,)), ((), ()))


def _const_spec(shape):
    nd = len(shape)
    return pl.BlockSpec(shape, lambda *_: (0,) * nd, pipeline_mode=pl.Buffered(1))


def _params(n_axes):
    return pltpu.CompilerParams(dimension_semantics=("parallel",) * n_axes,
                                vmem_limit_bytes=VMEM_LIMIT)


def _params_sequential():
    return pltpu.CompilerParams(dimension_semantics=("arbitrary",), vmem_limit_bytes=VMEM_LIMIT)


def _layer_norm(z, g, b):
    mu = jnp.mean(z, axis=-1, keepdims=True)
    zc = z - mu
    var = jnp.mean(zc * zc, axis=-1, keepdims=True)
    return zc * lax.rsqrt(var + LN_EPS) * g + b


def _lagged_row_specs(n_tiles, tm):
    in_spec = lambda width: pl.BlockSpec((tm, width), lambda i: (jnp.minimum(i, n_tiles - 1), 0))
    out_spec = lambda width: pl.BlockSpec((tm, width), lambda i: (jnp.maximum(i - 1, 0), 0))
    return in_spec, out_spec


def _lagged_norm(z_ref, g_ref, b_ref, o_ref):
    n_groups = z_ref.shape[1] // MXU_COLS
    rows = z_ref.shape[0] // n_groups
    zero_rows = []
    for j in range(n_groups):
        rs = slice(j * rows, (j + 1) * rows)
        out = _layer_norm(z_ref[rs, :], g_ref[...], b_ref[...])
        o_ref[rs, :] = out
        bits = lax.bitcast_convert_type(jnp.max(out, axis=0, keepdims=True)[:, :MXU_COLS], jnp.uint32)
        zero_rows.append(lax.bitcast_convert_type((bits >> 16) >> 16, F32))
    return zero_rows


def _run_lagged(tile_fn, z_ref, g_ref, b_ref, o_ref):
    step = pl.program_id(0)
    last = pl.num_programs(0) - 1

    @pl.when(step == 0)
    def _():
        z_ref[...] = jnp.zeros(z_ref.shape, F32)

    @pl.when(step < last)
    def _():
        zero_rows = _lagged_norm(z_ref, g_ref, b_ref, o_ref)
        z_ref[...] = tile_fn(zero_rows)

    @pl.when(step == last)
    def _():
        _lagged_norm(z_ref, g_ref, b_ref, o_ref)


def _ffn_ln_body(n_casts, x_ref, win_ref, wout_ref, g_ref, b_ref, *refs):
    cast_in, o_ref, cast_out = refs[:2 * n_casts], refs[2 * n_casts], refs[2 * n_casts + 1:3 * n_casts + 1]
    z_ref, h_ref = refs[3 * n_casts + 1:]

    def tile(zero_rows):
        for j in range(n_casts):
            cast_out[j][...] = (cast_in[2 * j][0] * cast_in[2 * j + 1][...]).astype(BF16)
        x = x_ref[...]
        xb = x.astype(BF16)
        for c in range(D_FF // MXU_COLS):
            cs = slice(c * MXU_COLS, (c + 1) * MXU_COLS)
            g = jnp.dot(xb, win_ref[:, cs], preferred_element_type=F32)
            u = jnp.dot(xb, win_ref[:, D_FF + c * MXU_COLS:D_FF + (c + 1) * MXU_COLS], preferred_element_type=F32)
            if c < len(zero_rows):
                g = g + zero_rows[c]
            h_ref[:, cs] = ((g + g * jnp.tanh(g)) * u).astype(BF16)
        y = jnp.dot(h_ref[...], wout_ref[...], preferred_element_type=F32)
        return DEEPNORM_ALPHA * x + 0.5 * y

    _run_lagged(tile, z_ref, g_ref, b_ref, o_ref)


def _ffn_ln(x2d, w_in, w_out, g, b, casts=()):
    t = x2d.shape[0]
    n_tiles = t // TM_FFN
    in_spec, out_spec = _lagged_row_specs(n_tiles, TM_FFN)
    cast_specs, cast_args, cast_out_specs, cast_out_shapes = [], [], [], []
    for src, scale in casts:
        _, rows, cols = src.shape
        blk = next(r for r in range(BF16_SUBLANES, rows + 1, BF16_SUBLANES)
                   if rows % r == 0 and rows // r <= n_tiles)
        last = rows // blk - 1
        cast_specs += [pl.BlockSpec((1, blk, cols), lambda i, last=last: (0, jnp.minimum(i, last), 0)),
                       _const_spec((1, cols))]
        cast_args += [src, scale]
        cast_out_specs.append(pl.BlockSpec((blk, cols), lambda i, last=last: (jnp.minimum(i, last), 0)))
        cast_out_shapes.append(jax.ShapeDtypeStruct((rows, cols), BF16))
    return pl.pallas_call(
        functools.partial(_ffn_ln_body, len(casts)),
        grid=(n_tiles + 1,),
        in_specs=[
            in_spec(D_MODEL),
            _const_spec((D_MODEL, 2 * D_FF)),
            _const_spec((D_FF, D_MODEL)),
            _const_spec((1, D_MODEL)),
            _const_spec((1, D_MODEL)),
        ] + cast_specs,
        out_specs=[out_spec(D_MODEL)] + cast_out_specs,
        out_shape=[jax.ShapeDtypeStruct((t, D_MODEL), F32)] + cast_out_shapes,
        scratch_shapes=[pltpu.VMEM((TM_FFN, D_MODEL), F32), pltpu.VMEM((TM_FFN, D_FF), BF16)],
        compiler_params=_params_sequential(),
        name="ffn_ln",
    )(x2d, w_in, w_out, g, b, *cast_args)


def _mem_kv_body(m_ref, w_ref, km_ref, vmt_ref):
    n_mem = m_ref.shape[0]
    kv = jnp.dot(m_ref[...].astype(BF16), w_ref[...], preferred_element_type=F32)
    k = kv[:, :MEM_WIDTH].astype(BF16)
    vt = kv[:, MEM_WIDTH:].T.astype(BF16)
    lane = lax.broadcasted_iota(jnp.int32, k.shape, 1)
    ones = jnp.ones((MEM_VT_ROWS - MEM_HEAD_DIM, n_mem), BF16)
    for hh in range(MEM_HEADS):
        km_ref[0, hh] = jnp.where(lane // MEM_HEAD_DIM == hh, k, jnp.zeros_like(k))
        vmt_ref[0, hh * MEM_VT_ROWS:hh * MEM_VT_ROWS + MEM_HEAD_DIM] = vt[hh * MEM_HEAD_DIM:(hh + 1) * MEM_HEAD_DIM]
        vmt_ref[0, hh * MEM_VT_ROWS + MEM_HEAD_DIM:(hh + 1) * MEM_VT_ROWS] = ones


def _mem_kv(mem2d, w_kv, n_mem):
    bsz = mem2d.shape[0] // n_mem
    return pl.pallas_call(
        _mem_kv_body,
        grid=(bsz,),
        in_specs=[
            pl.BlockSpec((n_mem, D_MODEL), lambda i: (i, 0)),
            _const_spec((D_MODEL, 2 * MEM_WIDTH)),
        ],
        out_specs=[pl.BlockSpec((1, MEM_HEADS, n_mem, MEM_WIDTH), lambda i: (i, 0, 0, 0)),
                   pl.BlockSpec((1, MEM_HEADS * MEM_VT_ROWS, n_mem), lambda i: (i, 0, 0))],
        out_shape=[jax.ShapeDtypeStruct((bsz, MEM_HEADS, n_mem, MEM_WIDTH), BF16),
                   jax.ShapeDtypeStruct((bsz, MEM_HEADS * MEM_VT_ROWS, n_mem), BF16)],
        compiler_params=_params(1),
        name="mem_kv",
    )(mem2d, w_kv)


def _proj_body(x_ref, pos_ref, invf_ref, w_ref, gmg_ref, gmb_ref, ws_ref, bst_ref, km_ref, vmt_ref,
               q_ref, k_ref, vt_ref, ygm_ref, ymem_ref):
    tm = x_ref.shape[0]
    xb = x_ref[...].astype(BF16)

    def project(off, width):
        return jnp.dot(xb, w_ref[:, off:off + width], preferred_element_type=F32)

    h_qm = project(OFF_QM, MEM_WIDTH)
    h_uv = project(OFF_U, 2 * GM_WIDTH)

    qm = (h_qm * (MEM_HEAD_DIM ** -0.5 * math.log2(math.e))).astype(BF16)
    st = [lax.dot_general(km_ref[0, hh], qm, NT_DIMS, preferred_element_type=F32) for hh in range(MEM_HEADS)]
    h_qk = project(OFF_Q, 2 * DIFF_Q_WIDTH)
    e = [jnp.exp2(s - jnp.max(s, axis=0, keepdims=True)).astype(BF16) for s in st]
    outs = []
    for hh in range(MEM_HEADS):
        ot = jnp.dot(vmt_ref[0, hh * MEM_VT_ROWS:(hh + 1) * MEM_VT_ROWS], e[hh], preferred_element_type=F32)
        outs.append(ot[:MEM_HEAD_DIM] * (1.0 / ot[MEM_HEAD_DIM:MEM_HEAD_DIM + 1]))
    ymem_ref[...] = jnp.concatenate(outs, axis=0).T.astype(BF16)

    u = jax.nn.gelu(h_uv[:, :GM_WIDTH])
    v = jax.nn.gelu(h_uv[:, GM_WIDTH:])
    vn = _layer_norm(v, gmg_ref[...], gmb_ref[...]).astype(BF16)
    row = lax.broadcasted_iota(jnp.int32, (GM_CHUNK, GM_CHUNK), 0)
    col = lax.broadcasted_iota(jnp.int32, (GM_CHUNK, GM_CHUNK), 1)
    causal = col <= row
    h_vd = project(OFF_VD, DIFF_V_WIDTH)
    for g in range(GM_GROUPS):
        wc = jnp.where(causal, ws_ref[g], 0.0).astype(BF16)
        bcol = bst_ref[:, g:g + 1]
        gs = slice(g * GM_GROUP_DIM, (g + 1) * GM_GROUP_DIM)
        for c in range(tm // GM_CHUNK):
            rs = slice(c * GM_CHUNK, (c + 1) * GM_CHUNK)
            mixed = jnp.dot(wc, vn[rs, gs], preferred_element_type=F32) + bcol
            ygm_ref[rs, gs] = (u[rs, gs] * mixed).astype(BF16)

    ang = invf_ref[...] * pos_ref[0].astype(F32)
    cos_t = jnp.cos(ang)
    sin_t = jnp.sin(ang)
    rest = DIFF_HEAD_DIM - ROPE_DIM
    one_t = jnp.ones((rest, tm), F32)
    zero_t = jnp.zeros((rest, tm), F32)
    zero_h = jnp.zeros((ROPE_HALF, tm), F32)
    cos = jnp.concatenate([cos_t, cos_t, one_t] * 2, axis=0).T
    sin_lo = jnp.concatenate([-sin_t, zero_h, zero_t] * 2, axis=0).T
    sin_hi = jnp.concatenate([zero_h, sin_t, zero_t] * 2, axis=0).T

    def rope(t):
        return (t * cos + pltpu.roll(t, LANES - ROPE_HALF, axis=1) * sin_lo
                + pltpu.roll(t, ROPE_HALF, axis=1) * sin_hi)

    scale = DIFF_HEAD_DIM ** -0.5 * math.log2(math.e)
    for j in range(DIFF_Q_WIDTH // LANES):
        cs = slice(j * LANES, (j + 1) * LANES)
        q_ref[:, cs] = (rope(h_qk[:, j * LANES:(j + 1) * LANES]) * scale).astype(BF16)
        k_ref[:, cs] = rope(h_qk[:, DIFF_Q_WIDTH + j * LANES:DIFF_Q_WIDTH + (j + 1) * LANES]).astype(BF16)
    vt = h_vd.T.astype(BF16)
    ones = jnp.ones((VT_ROWS - DIFF_V_DIM, tm), BF16)
    for hh in range(DIFF_HEADS):
        vt_ref[0, hh * VT_ROWS:hh * VT_ROWS + DIFF_V_DIM] = vt[hh * DIFF_V_DIM:(hh + 1) * DIFF_V_DIM]
        vt_ref[0, hh * VT_ROWS + DIFF_V_DIM:(hh + 1) * VT_ROWS] = ones


def _proj(x2d, pos3d, invf, w, gm_g, gm_b, w_s, b_st, km, vmt, seq, n_mem):
    t = x2d.shape[0]
    tiles_per_batch = seq // TM_PROJ
    row_spec = lambda width: pl.BlockSpec((TM_PROJ, width), lambda i: (i, 0))
    return pl.pallas_call(
        _proj_body,
        grid=(t // TM_PROJ,),
        in_specs=[
            row_spec(D_MODEL),
            pl.BlockSpec((1, 1, TM_PROJ), lambda i: (i, 0, 0)),
            _const_spec((ROPE_HALF, 1)),
            _const_spec((D_MODEL, IN_WIDTH)),
            _const_spec((1, GM_WIDTH)),
            _const_spec((1, GM_WIDTH)),
            _const_spec((GM_GROUPS, GM_CHUNK, GM_CHUNK)),
            _const_spec((GM_CHUNK, GM_GROUPS)),
            pl.BlockSpec((1, MEM_HEADS, n_mem, MEM_WIDTH), lambda i: (i // tiles_per_batch, 0, 0, 0)),
            pl.BlockSpec((1, MEM_HEADS * MEM_VT_ROWS, n_mem), lambda i: (i // tiles_per_batch, 0, 0)),
        ],
        out_specs=[row_spec(DIFF_Q_WIDTH), row_spec(DIFF_Q_WIDTH),
                   pl.BlockSpec((1, DIFF_HEADS * VT_ROWS, TM_PROJ),
                                lambda i: (i // tiles_per_batch, 0, i % tiles_per_batch)),
                   row_spec(GM_WIDTH), row_spec(MEM_WIDTH)],
        out_shape=[jax.ShapeDtypeStruct((t, DIFF_Q_WIDTH), BF16),
                   jax.ShapeDtypeStruct((t, DIFF_Q_WIDTH), BF16),
                   jax.ShapeDtypeStruct((t // seq, DIFF_HEADS * VT_ROWS, seq), BF16),
                   jax.ShapeDtypeStruct((t, GM_WIDTH), BF16),
                   jax.ShapeDtypeStruct((t, MEM_WIDTH), BF16)],
        compiler_params=_params(1),
        name="proj",
    )(x2d, pos3d, invf, w, gm_g, gm_b, w_s, b_st, km, vmt)


def _diff_attn_body(lamv_ref, ngc_ref, q_ref, k_ref, vt_ref, o_ref, qs_ref, st_ref, e_ref, acc_ref):
    qi = pl.program_id(1)
    heads = range(DIFF_HEADS)
    lane = lax.broadcasted_iota(jnp.int32, (TQ, DIFF_V_DIM), 1)
    for h in heads:
        q = q_ref[0, :, h * DIFF_V_DIM:(h + 1) * DIFF_V_DIM]
        zero = jnp.zeros_like(q)
        qs_ref[h, :TQ] = jnp.where(lane < DIFF_HEAD_DIM, q, zero)
        qs_ref[h, TQ:] = jnp.where(lane >= DIFF_HEAD_DIM, q, zero)

    def scores(h, ki, clear=False):
        kb = k_ref[0, pl.ds(pl.multiple_of(ki * TQ, TQ), TQ), h * DIFF_V_DIM:(h + 1) * DIFF_V_DIM]
        st = lax.dot_general(kb, qs_ref[h], NT_DIMS, preferred_element_type=F32)
        st_ref[h] = st
        if clear:
            zero = lax.bitcast_convert_type((lax.bitcast_convert_type(st, jnp.uint32) >> 16) >> 16, F32)
            e_ref[h] = zero.astype(BF16)
            acc_ref[h] = zero[:VT_ROWS]
        return jnp.max(st, axis=0, keepdims=True)

    def values(h, ki):
        vtb = vt_ref[0, h * VT_ROWS:(h + 1) * VT_ROWS, pl.ds(pl.multiple_of(ki * TQ, TQ), TQ)]
        return jnp.dot(vtb, e_ref[h], preferred_element_type=F32)

    def softmax(h, st, m, bm):
        m_new = jnp.maximum(m, bm)
        e_ref[h] = jnp.exp2(st - m_new).astype(BF16)
        return jnp.exp2(m - m_new), m_new

    def step(t, carry):
        a_prev, m, bm = carry
        pv = [values(h, jnp.maximum(t - 1, 0)) for h in heads]
        am = [softmax(h, st_ref[h], m[h], bm[h]) for h in heads]
        for h in heads:
            acc_ref[h] = a_prev[h] * acc_ref[h] + pv[h]
        return (tuple(x[0] for x in am), tuple(x[1] for x in am),
                tuple(scores(h, t + 1) for h in heads))

    init = (tuple(jnp.ones((1, 2 * TQ), F32) for _ in heads),
            tuple(jnp.full((1, 2 * TQ), -jnp.inf, F32) for _ in heads),
            tuple(scores(h, 0, clear=True) for h in heads))
    a_prev, m, _ = lax.fori_loop(0, qi, step, init)

    lv = lamv_ref[...]
    lam = (jnp.exp(jnp.sum(lv[0:1] * lv[1:2], axis=-1, keepdims=True))
           - jnp.exp(jnp.sum(lv[2:3] * lv[3:4], axis=-1, keepdims=True)) + LAM_INIT)
    kpos = lax.broadcasted_iota(jnp.int32, (TQ, 2 * TQ), 0)
    qpos = lax.broadcasted_iota(jnp.int32, (TQ, 2 * TQ), 1) % TQ
    visible = kpos <= qpos
    pv = [values(h, jnp.maximum(qi - 1, 0)) for h in heads]
    for h in heads:
        st = jnp.where(visible, st_ref[h], -jnp.inf)
        a, _ = softmax(h, st, m[h], jnp.max(st, axis=0, keepdims=True))
        acc = a * (a_prev[h] * acc_ref[h] + pv[h]) + values(h, qi)
        on = acc[:DIFF_V_DIM] * (1.0 / acc[DIFF_V_DIM:DIFF_V_DIM + 1])
        ot = on[:, :TQ] - lam * on[:, TQ:]
        ot = ot * lax.rsqrt(jnp.mean(ot * ot, axis=0, keepdims=True) + LN_EPS) * ngc_ref[...]
        o_ref[0, :, h * DIFF_V_DIM:(h + 1) * DIFF_V_DIM] = (ot * (1.0 - LAM_INIT)).T.astype(BF16)


def _diff_attn(lamv, norm_g_col, q, k, vt):
    b, s, _ = q.shape
    return pl.pallas_call(
        _diff_attn_body,
        grid=(b, s // TQ),
        in_specs=[
            _const_spec((4, DIFF_HEAD_DIM)),
            _const_spec((DIFF_V_DIM, 1)),
            pl.BlockSpec((1, TQ, DIFF_Q_WIDTH), lambda bi, qi: (bi, qi, 0)),
            pl.BlockSpec((1, s, DIFF_Q_WIDTH), lambda bi, qi: (bi, 0, 0)),
            pl.BlockSpec((1, DIFF_HEADS * VT_ROWS, s), lambda bi, qi: (bi, 0, 0)),
        ],
        out_specs=pl.BlockSpec((1, TQ, DIFF_V_WIDTH), lambda bi, qi: (bi, qi, 0)),
        out_shape=jax.ShapeDtypeStruct((b, s, DIFF_V_WIDTH), BF16),
        scratch_shapes=[pltpu.VMEM((DIFF_HEADS, 2 * TQ, DIFF_V_DIM), BF16),
                        pltpu.VMEM((DIFF_HEADS, TQ, 2 * TQ), F32),
                        pltpu.VMEM((DIFF_HEADS, TQ, 2 * TQ), BF16),
                        pltpu.VMEM((DIFF_HEADS, VT_ROWS, 2 * TQ), F32)],
        compiler_params=_params(2),
        name="diff_attn",
    )(lamv, norm_g_col, q, k, vt)


def _merge_body(x_ref, ygm_ref, ydf_ref, ymm_ref, wgate_ref, gb_ref, wgm_ref, wdf_ref, wmm_ref, wo_ref,
                g_ref, b_ref, o_ref, z_ref, m_ref):
    def tile(zero_rows):
        x = x_ref[...]
        xb = x.astype(BF16)
        branch_refs = ((ygm_ref, wgm_ref), (ydf_ref, wdf_ref), (ymm_ref, wmm_ref))
        for c in range(D_MODEL // MXU_COLS):
            cs = slice(c * MXU_COLS, (c + 1) * MXU_COLS)
            merged = zero_rows[c]
            for r, (y_ref, w_ref) in enumerate(branch_refs):
                gs = slice(r * D_MODEL + c * MXU_COLS, r * D_MODEL + (c + 1) * MXU_COLS)
                y = jnp.dot(y_ref[...], w_ref[:, cs], preferred_element_type=F32)
                logits = jnp.dot(xb, wgate_ref[:, OFF_GATE + gs.start:OFF_GATE + gs.stop],
                                 preferred_element_type=F32)
                merged = merged + (y + jnp.tanh(logits + gb_ref[:, gs]) * y)
            m_ref[:, cs] = merged.astype(BF16)
        y = jnp.dot(m_ref[...], wo_ref[...], preferred_element_type=F32)
        return DEEPNORM_ALPHA * x + y

    _run_lagged(tile, z_ref, g_ref, b_ref, o_ref)


def _merge(x2d, ygm, ydf, ymm, w_gate, gate_b, w_gm, w_df, w_mm, w_o, g, b):
    t = x2d.shape[0]
    n_tiles = t // TM_MERGE
    row_spec, out_spec = _lagged_row_specs(n_tiles, TM_MERGE)
    return pl.pallas_call(
        _merge_body,
        grid=(n_tiles + 1,),
        in_specs=[
            row_spec(D_MODEL), row_spec(GM_WIDTH), row_spec(DIFF_V_WIDTH), row_spec(MEM_WIDTH),
            _const_spec((D_MODEL, IN_WIDTH)),
            _const_spec((1, N_BRANCH * D_MODEL)),
            _const_spec((GM_WIDTH, D_MODEL)),
            _const_spec((DIFF_V_WIDTH, D_MODEL)),
            _const_spec((MEM_WIDTH, D_MODEL)),
            _const_spec((D_MODEL, D_MODEL)),
            _const_spec((1, D_MODEL)),
            _const_spec((1, D_MODEL)),
        ],
        out_specs=out_spec(D_MODEL),
        out_shape=jax.ShapeDtypeStruct((t, D_MODEL), F32),
        scratch_shapes=[pltpu.VMEM((TM_MERGE, D_MODEL), F32), pltpu.VMEM((TM_MERGE, D_MODEL), BF16)],
        compiler_params=_params_sequential(),
        name="merge",
    )(x2d, ygm, ydf, ymm, w_gate, gate_b, w_gm, w_df, w_mm, w_o, g, b)


def _rope_freqs():
    return (ROPE_THETA ** (-jnp.arange(0, ROPE_DIM, 2, dtype=F32) / ROPE_DIM)).reshape(ROPE_HALF, 1)


def kernel(x, mem, positions, ffn1_w_in, ffn1_w_out, ln1_g, ln1_b, w_in, gate_b, gm_ln_g, gm_ln_b, gm_w_s, gm_b_s,
           lambda_q1, lambda_k1, lambda_q2, lambda_k2, diff_norm_g, w_mem_kv, w_branch_gm, w_branch_diff,
           w_branch_mem, w_o, ln2_g, ln2_b, ffn2_w_in, ffn2_w_out, ln3_g, ln3_b):
    bsz, seq, _ = x.shape
    n_mem = mem.shape[1]
    t = bsz * seq
    i = 0
    x2d = x.reshape(t, D_MODEL)

    ffn_scale = jnp.concatenate([jnp.full((1, D_FF), 0.5, F32), jnp.ones((1, D_FF), F32)], axis=1)
    in_scale = jnp.concatenate([jnp.ones((1, OFF_GATE), F32), jnp.full((1, N_BRANCH * D_MODEL), 0.5, F32)], axis=1)

    x1, w_in_b, ffn2_w_in_b, ffn2_w_out_b = _ffn_ln(
        x2d, (ffn1_w_in[i] * ffn_scale).astype(BF16), ffn1_w_out[i].astype(BF16), ln1_g[i][None], ln1_b[i][None],
        casts=((w_in[i:i + 1], in_scale), (ffn2_w_in[i:i + 1], ffn_scale),
               (ffn2_w_out[i:i + 1], jnp.ones((1, D_MODEL), F32))))

    km, vmt = _mem_kv(mem.reshape(bsz * n_mem, D_MODEL), w_mem_kv[i].astype(BF16), n_mem)
    q, k, vt, ygm, ymm = _proj(
        x1, positions.reshape(t // TM_PROJ, 1, TM_PROJ), _rope_freqs(), w_in_b,
        gm_ln_g[i][None], gm_ln_b[i][None], gm_w_s[i], gm_b_s[i].T, km, vmt, seq, n_mem)

    lamv = jnp.stack([lambda_q1[i], lambda_k1[i], lambda_q2[i], lambda_k2[i]]).astype(F32)
    ydf = _diff_attn(lamv, diff_norm_g[i][:, None],
                     q.reshape(bsz, seq, DIFF_Q_WIDTH), k.reshape(bsz, seq, DIFF_Q_WIDTH),
                     vt).reshape(t, DIFF_V_WIDTH)

    x2 = _merge(x1, ygm, ydf, ymm, w_in_b, 0.5 * gate_b[i][None],
                w_branch_gm[i].astype(BF16), w_branch_diff[i].astype(BF16), w_branch_mem[i].astype(BF16),
                (0.5 * w_o[i]).astype(BF16), ln2_g[i][None], ln2_b[i][None])

    x3, = _ffn_ln(x2, ffn2_w_in_b, ffn2_w_out_b, ln3_g[i][None], ln3_b[i][None])
    return x3.reshape(bsz, seq, D_MODEL)
```

```python
import functools
import math

import jax
import jax.numpy as jnp
from jax import lax
from jax.experimental import pallas as pl
from jax.experimental.pallas import tpu as pltpu

F32 = jnp.float32
BF16 = jnp.bfloat16

D_MODEL = 1024
D_FF = 2816
GM_WIDTH = 512
GM_GROUPS = 4
GM_GROUP_DIM = GM_WIDTH // GM_GROUPS
GM_CHUNK = 128
DIFF_HEADS = 4
DIFF_HEAD_DIM = 64
DIFF_V_DIM = 2 * DIFF_HEAD_DIM
DIFF_Q_WIDTH = DIFF_HEADS * 2 * DIFF_HEAD_DIM
DIFF_V_WIDTH = DIFF_HEADS * DIFF_V_DIM
MEM_HEADS = 4
MEM_HEAD_DIM = 64
MEM_WIDTH = MEM_HEADS * MEM_HEAD_DIM
N_BRANCH = 3
ROPE_THETA = 500000.0
ROPE_DIM = DIFF_HEAD_DIM // 4
ROPE_HALF = ROPE_DIM // 2
DEPTH = 1
DEEPNORM_ALPHA = (2 * DEPTH) ** 0.25
LN_EPS = 1e-5
LAM_INIT = 0.8 - 0.6 * math.exp(-0.3 * 0)
OFF_U = 0
OFF_V = OFF_U + GM_WIDTH
OFF_Q = OFF_V + GM_WIDTH
OFF_K = OFF_Q + DIFF_Q_WIDTH
OFF_VD = OFF_K + DIFF_Q_WIDTH
OFF_QM = OFF_VD + DIFF_V_WIDTH
OFF_GATE = OFF_QM + MEM_WIDTH
IN_WIDTH = OFF_GATE + N_BRANCH * D_MODEL

LANES = 128
BF16_SUBLANES = 16
VT_ROWS = DIFF_V_DIM + BF16_SUBLANES
MEM_VT_ROWS = MEM_HEAD_DIM + BF16_SUBLANES
MXU_COLS = 256
VMEM_LIMIT = 52 * 1024 * 1024

TM_FFN = 1024
TM_PROJ = 512
TM_MERGE = 1024
TQ = 512

NT_DIMS = (((1,), (1,)), ((), ()))


def _const_spec(shape):
    nd = len(shape)
    return pl.BlockSpec(shape, lambda *_: (0,) * nd, pipeline_mode=pl.Buffered(1))


def _params(n_axes):
    return pltpu.CompilerParams(dimension_semantics=("parallel",) * n_axes,
                                vmem_limit_bytes=VMEM_LIMIT)


def _params_sequential():
    return pltpu.CompilerParams(dimension_semantics=("arbitrary",), vmem_limit_bytes=VMEM_LIMIT)


def _layer_norm(z, g, b):
    mu = jnp.mean(z, axis=-1, keepdims=True)
    zc = z - mu
    var = jnp.mean(zc * zc, axis=-1, keepdims=True)
    return zc * lax.rsqrt(var + LN_EPS) * g + b


def _cast_jobs(casts, n_steps):
    in_specs, args, out_specs, out_shapes = [], [], [], []
    for src, scale in casts:
        _, rows, cols = src.shape
        blk = next(r for r in range(BF16_SUBLANES, rows + 1, BF16_SUBLANES)
                   if rows % r == 0 and rows // r <= n_steps)
        last = rows // blk - 1
        in_specs += [pl.BlockSpec((1, blk, cols), lambda i, last=last: (0, jnp.minimum(i, last), 0)),
                     _const_spec((1, cols))]
        args += [src, scale]
        out_specs.append(pl.BlockSpec((blk, cols), lambda i, last=last: (jnp.minimum(i, last), 0)))
        out_shapes.append(jax.ShapeDtypeStruct((rows, cols), BF16))
    return in_specs, args, out_specs, out_shapes


def _run_casts(cast_in, cast_out):
    for j, out_ref in enumerate(cast_out):
        out_ref[...] = (cast_in[2 * j][0] * cast_in[2 * j + 1][...]).astype(BF16)


def _lagged_row_specs(n_tiles, tm):
    in_spec = lambda width: pl.BlockSpec((tm, width), lambda i: (jnp.minimum(i, n_tiles - 1), 0))
    out_spec = lambda width: pl.BlockSpec((tm, width), lambda i: (jnp.maximum(i - 1, 0), 0))
    return in_spec, out_spec


def _lagged_norm(z_ref, g_ref, b_ref, o_ref):
    n_groups = z_ref.shape[1] // MXU_COLS
    rows = z_ref.shape[0] // n_groups
    zero_rows = []
    for j in range(n_groups):
        rs = slice(j * rows, (j + 1) * rows)
        out = _layer_norm(z_ref[rs, :], g_ref[...], b_ref[...])
        o_ref[rs, :] = out
        bits = lax.bitcast_convert_type(jnp.max(out, axis=0, keepdims=True)[:, :MXU_COLS], jnp.uint32)
        zero_rows.append(lax.bitcast_convert_type((bits >> 16) >> 16, F32))
    return zero_rows


def _run_lagged(tile_fn, z_ref, g_ref, b_ref, o_ref):
    step = pl.program_id(0)
    last = pl.num_programs(0) - 1

    @pl.when(step == 0)
    def _():
        z_ref[...] = jnp.zeros(z_ref.shape, F32)

    @pl.when(step < last)
    def _():
        zero_rows = _lagged_norm(z_ref, g_ref, b_ref, o_ref)
        z_ref[...] = tile_fn(zero_rows)

    @pl.when(step == last)
    def _():
        _lagged_norm(z_ref, g_ref, b_ref, o_ref)


def _ffn_ln_body(n_casts, x_ref, win_ref, wout_ref, g_ref, b_ref, *refs):
    cast_in, o_ref, cast_out = refs[:2 * n_casts], refs[2 * n_casts], refs[2 * n_casts + 1:3 * n_casts + 1]
    z_ref, h_ref = refs[3 * n_casts + 1:]

    def tile(zero_rows):
        _run_casts(cast_in, cast_out)
        x = x_ref[...]
        xb = x.astype(BF16)
        for c in range(D_FF // MXU_COLS):
            cs = slice(c * MXU_COLS, (c + 1) * MXU_COLS)
            g = jnp.dot(xb, win_ref[:, cs], preferred_element_type=F32)
            u = jnp.dot(xb, win_ref[:, D_FF + c * MXU_COLS:D_FF + (c + 1) * MXU_COLS], preferred_element_type=F32)
            if c < len(zero_rows):
                g = g + zero_rows[c]
            h_ref[:, cs] = ((g + g * jnp.tanh(g)) * u).astype(BF16)
        y = jnp.dot(h_ref[...], wout_ref[...], preferred_element_type=F32)
        return DEEPNORM_ALPHA * x + 0.5 * y

    _run_lagged(tile, z_ref, g_ref, b_ref, o_ref)


def _ffn_ln(x2d, w_in, w_out, g, b, casts=()):
    t = x2d.shape[0]
    n_tiles = t // TM_FFN
    in_spec, out_spec = _lagged_row_specs(n_tiles, TM_FFN)
    cast_specs, cast_args, cast_out_specs, cast_out_shapes = _cast_jobs(casts, n_tiles)
    return pl.pallas_call(
        functools.partial(_ffn_ln_body, len(casts)),
        grid=(n_tiles + 1,),
        in_specs=[
            in_spec(D_MODEL),
            _const_spec((D_MODEL, 2 * D_FF)),
            _const_spec((D_FF, D_MODEL)),
            _const_spec((1, D_MODEL)),
            _const_spec((1, D_MODEL)),
        ] + cast_specs,
        out_specs=[out_spec(D_MODEL)] + cast_out_specs,
        out_shape=[jax.ShapeDtypeStruct((t, D_MODEL), F32)] + cast_out_shapes,
        scratch_shapes=[pltpu.VMEM((TM_FFN, D_MODEL), F32), pltpu.VMEM((TM_FFN, D_FF), BF16)],
        compiler_params=_params_sequential(),
        name="ffn_ln",
    )(x2d, w_in, w_out, g, b, *cast_args)


def _mem_kv_body(n_casts, m_ref, w_ref, *refs):
    cast_in, (km_ref, vmt_ref), cast_out = refs[:2 * n_casts], refs[2 * n_casts:2 * n_casts + 2], refs[2 * n_casts + 2:]
    _run_casts(cast_in, cast_out)
    n_mem = m_ref.shape[0]
    kv = jnp.dot(m_ref[...].astype(BF16), w_ref[...], preferred_element_type=F32)
    k = kv[:, :MEM_WIDTH].astype(BF16)
    vt = kv[:, MEM_WIDTH:].T.astype(BF16)
    lane = lax.broadcasted_iota(jnp.int32, k.shape, 1)
    ones = jnp.ones((MEM_VT_ROWS - MEM_HEAD_DIM, n_mem), BF16)
    for hh in range(MEM_HEADS):
        km_ref[0, hh] = jnp.where(lane // MEM_HEAD_DIM == hh, k, jnp.zeros_like(k))
        vmt_ref[0, hh * MEM_VT_ROWS:hh * MEM_VT_ROWS + MEM_HEAD_DIM] = vt[hh * MEM_HEAD_DIM:(hh + 1) * MEM_HEAD_DIM]
        vmt_ref[0, hh * MEM_VT_ROWS + MEM_HEAD_DIM:(hh + 1) * MEM_VT_ROWS] = ones


def _mem_kv(mem2d, w_kv, n_mem, casts=()):
    bsz = mem2d.shape[0] // n_mem
    cast_specs, cast_args, cast_out_specs, cast_out_shapes = _cast_jobs(casts, bsz)
    return pl.pallas_call(
        functools.partial(_mem_kv_body, len(casts)),
        grid=(bsz,),
        in_specs=[
            pl.BlockSpec((n_mem, D_MODEL), lambda i: (i, 0)),
            _const_spec((D_MODEL, 2 * MEM_WIDTH)),
        ] + cast_specs,
        out_specs=[pl.BlockSpec((1, MEM_HEADS, n_mem, MEM_WIDTH), lambda i: (i, 0, 0, 0)),
                   pl.BlockSpec((1, MEM_HEADS * MEM_VT_ROWS, n_mem), lambda i: (i, 0, 0))] + cast_out_specs,
        out_shape=[jax.ShapeDtypeStruct((bsz, MEM_HEADS, n_mem, MEM_WIDTH), BF16),
                   jax.ShapeDtypeStruct((bsz, MEM_HEADS * MEM_VT_ROWS, n_mem), BF16)] + cast_out_shapes,
        compiler_params=_params(1),
        name="mem_kv",
    )(mem2d, w_kv, *cast_args)


def _proj_body(x_ref, pos_ref, invf_ref, w_ref, gmg_ref, gmb_ref, ws_ref, bst_ref, km_ref, vmt_ref,
               q_ref, k_ref, vt_ref, ygm_ref, ymem_ref):
    tm = x_ref.shape[0]
    xb = x_ref[...].astype(BF16)

    def project(off, width):
        return jnp.dot(xb, w_ref[:, off:off + width], preferred_element_type=F32)

    h_qm = project(OFF_QM, MEM_WIDTH)
    h_uv = project(OFF_U, 2 * GM_WIDTH)

    qm = (h_qm * (MEM_HEAD_DIM ** -0.5 * math.log2(math.e))).astype(BF16)
    st = [lax.dot_general(km_ref[0, hh], qm, NT_DIMS, preferred_element_type=F32) for hh in range(MEM_HEADS)]
    h_qk = project(OFF_Q, 2 * DIFF_Q_WIDTH)
    e = [jnp.exp2(s - jnp.max(s, axis=0, keepdims=True)).astype(BF16) for s in st]
    outs = []
    for hh in range(MEM_HEADS):
        ot = jnp.dot(vmt_ref[0, hh * MEM_VT_ROWS:(hh + 1) * MEM_VT_ROWS], e[hh], preferred_element_type=F32)
        outs.append(ot[:MEM_HEAD_DIM] * (1.0 / ot[MEM_HEAD_DIM:MEM_HEAD_DIM + 1]))
    ymem_ref[...] = jnp.concatenate(outs, axis=0).T.astype(BF16)

    u = jax.nn.gelu(h_uv[:, :GM_WIDTH])
    v = jax.nn.gelu(h_uv[:, GM_WIDTH:])
    vn = _layer_norm(v, gmg_ref[...], gmb_ref[...]).astype(BF16)
    row = lax.broadcasted_iota(jnp.int32, (GM_CHUNK, GM_CHUNK), 0)
    col = lax.broadcasted_iota(jnp.int32, (GM_CHUNK, GM_CHUNK), 1)
    causal = col <= row
    h_vd = project(OFF_VD, DIFF_V_WIDTH)
    for g in range(GM_GROUPS):
        wc = jnp.where(causal, ws_ref[g], 0.0).astype(BF16)
        bcol = bst_ref[:, g:g + 1]
        gs = slice(g * GM_GROUP_DIM, (g + 1) * GM_GROUP_DIM)
        for c in range(tm // GM_CHUNK):
            rs = slice(c * GM_CHUNK, (c + 1) * GM_CHUNK)
            mixed = jnp.dot(wc, vn[rs, gs], preferred_element_type=F32) + bcol
            ygm_ref[rs, gs] = (u[rs, gs] * mixed).astype(BF16)

    ang = invf_ref[...] * pos_ref[0].astype(F32)
    cos_t = jnp.cos(ang)
    sin_t = jnp.sin(ang)
    rest = DIFF_HEAD_DIM - ROPE_DIM
    one_t = jnp.ones((rest, tm), F32)
    zero_t = jnp.zeros((rest, tm), F32)
    zero_h = jnp.zeros((ROPE_HALF, tm), F32)
    cos = jnp.concatenate([cos_t, cos_t, one_t] * 2, axis=0).T
    sin_lo = jnp.concatenate([-sin_t, zero_h, zero_t] * 2, axis=0).T
    sin_hi = jnp.concatenate([zero_h, sin_t, zero_t] * 2, axis=0).T

    def rope(t):
        return (t * cos + pltpu.roll(t, LANES - ROPE_HALF, axis=1) * sin_lo
                + pltpu.roll(t, ROPE_HALF, axis=1) * sin_hi)

    scale = DIFF_HEAD_DIM ** -0.5 * math.log2(math.e)
    for j in range(DIFF_Q_WIDTH // LANES):
        cs = slice(j * LANES, (j + 1) * LANES)
        q_ref[:, cs] = (rope(h_qk[:, j * LANES:(j + 1) * LANES]) * scale).astype(BF16)
        k_ref[:, cs] = rope(h_qk[:, DIFF_Q_WIDTH + j * LANES:DIFF_Q_WIDTH + (j + 1) * LANES]).astype(BF16)
    vt = h_vd.T.astype(BF16)
    ones = jnp.ones((VT_ROWS - DIFF_V_DIM, tm), BF16)
    for hh in range(DIFF_HEADS):
        vt_ref[0, hh * VT_ROWS:hh * VT_ROWS + DIFF_V_DIM] = vt[hh * DIFF_V_DIM:(hh + 1) * DIFF_V_DIM]
        vt_ref[0, hh * VT_ROWS + DIFF_V_DIM:(hh + 1) * VT_ROWS] = ones


def _proj(x2d, pos3d, invf, w, gm_g, gm_b, w_s, b_st, km, vmt, seq, n_mem):
    t = x2d.shape[0]
    tiles_per_batch = seq // TM_PROJ
    row_spec = lambda width: pl.BlockSpec((TM_PROJ, width), lambda i: (i, 0))
    return pl.pallas_call(
        _proj_body,
        grid=(t // TM_PROJ,),
        in_specs=[
            row_spec(D_MODEL),
            pl.BlockSpec((1, 1, TM_PROJ), lambda i: (i, 0, 0)),
            _const_spec((ROPE_HALF, 1)),
            _const_spec((D_MODEL, IN_WIDTH)),
            _const_spec((1, GM_WIDTH)),
            _const_spec((1, GM_WIDTH)),
            _const_spec((GM_GROUPS, GM_CHUNK, GM_CHUNK)),
            _const_spec((GM_CHUNK, GM_GROUPS)),
            pl.BlockSpec((1, MEM_HEADS, n_mem, MEM_WIDTH), lambda i: (i // tiles_per_batch, 0, 0, 0)),
            pl.BlockSpec((1, MEM_HEADS * MEM_VT_ROWS, n_mem), lambda i: (i // tiles_per_batch, 0, 0)),
        ],
        out_specs=[row_spec(DIFF_Q_WIDTH), row_spec(DIFF_Q_WIDTH),
                   pl.BlockSpec((1, DIFF_HEADS * VT_ROWS, TM_PROJ),
                                lambda i: (i // tiles_per_batch, 0, i % tiles_per_batch)),
                   row_spec(GM_WIDTH), row_spec(MEM_WIDTH)],
        out_shape=[jax.ShapeDtypeStruct((t, DIFF_Q_WIDTH), BF16),
                   jax.ShapeDtypeStruct((t, DIFF_Q_WIDTH), BF16),
                   jax.ShapeDtypeStruct((t // seq, DIFF_HEADS * VT_ROWS, seq), BF16),
                   jax.ShapeDtypeStruct((t, GM_WIDTH), BF16),
                   jax.ShapeDtypeStruct((t, MEM_WIDTH), BF16)],
        compiler_params=_params(1),
        name="proj",
    )(x2d, pos3d, invf, w, gm_g, gm_b, w_s, b_st, km, vmt)


def _diff_attn_body(lamv_ref, ngc_ref, q_ref, k_ref, vt_ref, o_ref, qs_ref, st_ref, e_ref, acc_ref):
    qi = pl.program_id(1)
    heads = range(DIFF_HEADS)
    lane = lax.broadcasted_iota(jnp.int32, (TQ, DIFF_V_DIM), 1)
    for h in heads:
        q = q_ref[0, :, h * DIFF_V_DIM:(h + 1) * DIFF_V_DIM]
        zero = jnp.zeros_like(q)
        qs_ref[h, :TQ] = jnp.where(lane < DIFF_HEAD_DIM, q, zero)
        qs_ref[h, TQ:] = jnp.where(lane >= DIFF_HEAD_DIM, q, zero)

    def scores(h, ki, clear=False):
        kb = k_ref[0, pl.ds(pl.multiple_of(ki * TQ, TQ), TQ), h * DIFF_V_DIM:(h + 1) * DIFF_V_DIM]
        st = lax.dot_general(kb, qs_ref[h], NT_DIMS, preferred_element_type=F32)
        st_ref[h] = st
        if clear:
            zero = lax.bitcast_convert_type((lax.bitcast_convert_type(st, jnp.uint32) >> 16) >> 16, F32)
            e_ref[h] = zero.astype(BF16)
            acc_ref[h] = zero[:VT_ROWS]
        return jnp.max(st, axis=0, keepdims=True)

    def values(h, ki):
        vtb = vt_ref[0, h * VT_ROWS:(h + 1) * VT_ROWS, pl.ds(pl.multiple_of(ki * TQ, TQ), TQ)]
        return jnp.dot(vtb, e_ref[h], preferred_element_type=F32)

    def softmax(h, st, m, bm):
        m_new = jnp.maximum(m, bm)
        e_ref[h] = jnp.exp2(st - m_new).astype(BF16)
        return jnp.exp2(m - m_new), m_new

    def step(t, carry):
        a_prev, m, bm = carry
        pv = [values(h, jnp.maximum(t - 1, 0)) for h in heads]
        am = [softmax(h, st_ref[h], m[h], bm[h]) for h in heads]
        for h in heads:
            acc_ref[h] = a_prev[h] * acc_ref[h] + pv[h]
        return (tuple(x[0] for x in am), tuple(x[1] for x in am),
                tuple(scores(h, t + 1) for h in heads))

    init = (tuple(jnp.ones((1, 2 * TQ), F32) for _ in heads),
            tuple(jnp.full((1, 2 * TQ), -jnp.inf, F32) for _ in heads),
            tuple(scores(h, 0, clear=True) for h in heads))
    a_prev, m, _ = lax.fori_loop(0, qi, step, init)

    lv = lamv_ref[...]
    lam = (jnp.exp(jnp.sum(lv[0:1] * lv[1:2], axis=-1, keepdims=True))
           - jnp.exp(jnp.sum(lv[2:3] * lv[3:4], axis=-1, keepdims=True)) + LAM_INIT)
    kpos = lax.broadcasted_iota(jnp.int32, (TQ, 2 * TQ), 0)
    qpos = lax.broadcasted_iota(jnp.int32, (TQ, 2 * TQ), 1) % TQ
    visible = kpos <= qpos
    pv = [values(h, jnp.maximum(qi - 1, 0)) for h in heads]
    for h in heads:
        st = jnp.where(visible, st_ref[h], -jnp.inf)
        a, _ = softmax(h, st, m[h], jnp.max(st, axis=0, keepdims=True))
        acc = a * (a_prev[h] * acc_ref[h] + pv[h]) + values(h, qi)
        on = acc[:DIFF_V_DIM] * (1.0 / acc[DIFF_V_DIM:DIFF_V_DIM + 1])
        ot = on[:, :TQ] - lam * on[:, TQ:]
        ot = ot * lax.rsqrt(jnp.mean(ot * ot, axis=0, keepdims=True) + LN_EPS) * ngc_ref[...]
        o_ref[0, :, h * DIFF_V_DIM:(h + 1) * DIFF_V_DIM] = (ot * (1.0 - LAM_INIT)).T.astype(BF16)


def _diff_attn(lamv, norm_g_col, q, k, vt):
    b, s, _ = q.shape
    return pl.pallas_call(
        _diff_attn_body,
        grid=(b, s // TQ),
        in_specs=[
            _const_spec((4, DIFF_HEAD_DIM)),
            _const_spec((DIFF_V_DIM, 1)),
            pl.BlockSpec((1, TQ, DIFF_Q_WIDTH), lambda bi, qi: (bi, qi, 0)),
            pl.BlockSpec((1, s, DIFF_Q_WIDTH), lambda bi, qi: (bi, 0, 0)),
            pl.BlockSpec((1, DIFF_HEADS * VT_ROWS, s), lambda bi, qi: (bi, 0, 0)),
        ],
        out_specs=pl.BlockSpec((1, TQ, DIFF_V_WIDTH), lambda bi, qi: (bi, qi, 0)),
        out_shape=jax.ShapeDtypeStruct((b, s, DIFF_V_WIDTH), BF16),
        scratch_shapes=[pltpu.VMEM((DIFF_HEADS, 2 * TQ, DIFF_V_DIM), BF16),
                        pltpu.VMEM((DIFF_HEADS, TQ, 2 * TQ), F32),
                        pltpu.VMEM((DIFF_HEADS, TQ, 2 * TQ), BF16),
                        pltpu.VMEM((DIFF_HEADS, VT_ROWS, 2 * TQ), F32)],
        compiler_params=_params(2),
        name="diff_attn",
    )(lamv, norm_g_col, q, k, vt)


def _merge_body(x_ref, ygm_ref, ydf_ref, ymm_ref, wgate_ref, gb_ref, wgm_ref, wdf_ref, wmm_ref, wo_ref,
                g_ref, b_ref, o_ref, z_ref, m_ref):
    def tile(zero_rows):
        x = x_ref[...]
        xb = x.astype(BF16)
        branch_refs = ((ygm_ref, wgm_ref), (ydf_ref, wdf_ref), (ymm_ref, wmm_ref))
        for c in range(D_MODEL // MXU_COLS):
            cs = slice(c * MXU_COLS, (c + 1) * MXU_COLS)
            merged = zero_rows[c]
            for r, (y_ref, w_ref) in enumerate(branch_refs):
                gs = slice(r * D_MODEL + c * MXU_COLS, r * D_MODEL + (c + 1) * MXU_COLS)
                y = jnp.dot(y_ref[...], w_ref[:, cs], preferred_element_type=F32)
                logits = jnp.dot(xb, wgate_ref[:, OFF_GATE + gs.start:OFF_GATE + gs.stop],
                                 preferred_element_type=F32)
                merged = merged + (y + jnp.tanh(logits + gb_ref[:, gs]) * y)
            m_ref[:, cs] = merged.astype(BF16)
        y = jnp.dot(m_ref[...], wo_ref[...], preferred_element_type=F32)
        return DEEPNORM_ALPHA * x + y

    _run_lagged(tile, z_ref, g_ref, b_ref, o_ref)


def _merge(x2d, ygm, ydf, ymm, w_gate, gate_b, w_gm, w_df, w_mm, w_o, g, b):
    t = x2d.shape[0]
    n_tiles = t // TM_MERGE
    row_spec, out_spec = _lagged_row_specs(n_tiles, TM_MERGE)
    return pl.pallas_call(
        _merge_body,
        grid=(n_tiles + 1,),
        in_specs=[
            row_spec(D_MODEL), row_spec(GM_WIDTH), row_spec(DIFF_V_WIDTH), row_spec(MEM_WIDTH),
            _const_spec((D_MODEL, IN_WIDTH)),
            _const_spec((1, N_BRANCH * D_MODEL)),
            _const_spec((GM_WIDTH, D_MODEL)),
            _const_spec((DIFF_V_WIDTH, D_MODEL)),
            _const_spec((MEM_WIDTH, D_MODEL)),
            _const_spec((D_MODEL, D_MODEL)),
            _const_spec((1, D_MODEL)),
            _const_spec((1, D_MODEL)),
        ],
        out_specs=out_spec(D_MODEL),
        out_shape=jax.ShapeDtypeStruct((t, D_MODEL), F32),
        scratch_shapes=[pltpu.VMEM((TM_MERGE, D_MODEL), F32), pltpu.VMEM((TM_MERGE, D_MODEL), BF16)],
        compiler_params=_params_sequential(),
        name="merge",
    )(x2d, ygm, ydf, ymm, w_gate, gate_b, w_gm, w_df, w_mm, w_o, g, b)


def _rope_freqs():
    return (ROPE_THETA ** (-jnp.arange(0, ROPE_DIM, 2, dtype=F32) / ROPE_DIM)).reshape(ROPE_HALF, 1)


def kernel(x, mem, positions, ffn1_w_in, ffn1_w_out, ln1_g, ln1_b, w_in, gate_b, gm_ln_g, gm_ln_b, gm_w_s, gm_b_s,
           lambda_q1, lambda_k1, lambda_q2, lambda_k2, diff_norm_g, w_mem_kv, w_branch_gm, w_branch_diff,
           w_branch_mem, w_o, ln2_g, ln2_b, ffn2_w_in, ffn2_w_out, ln3_g, ln3_b):
    bsz, seq, _ = x.shape
    n_mem = mem.shape[1]
    t = bsz * seq
    i = 0
    x2d = x.reshape(t, D_MODEL)

    ffn_scale = jnp.concatenate([jnp.full((1, D_FF), 0.5, F32), jnp.ones((1, D_FF), F32)], axis=1)
    in_scale = jnp.concatenate([jnp.ones((1, OFF_GATE), F32), jnp.full((1, N_BRANCH * D_MODEL), 0.5, F32)], axis=1)

    no_scale = jnp.ones((1, D_MODEL), F32)
    km, vmt, ffn1_w_in_b, ffn1_w_out_b = _mem_kv(
        mem.reshape(bsz * n_mem, D_MODEL), w_mem_kv[i].astype(BF16), n_mem,
        casts=((ffn1_w_in[i:i + 1], ffn_scale), (ffn1_w_out[i:i + 1], no_scale)))
    x1, w_in_b, ffn2_w_in_b, ffn2_w_out_b = _ffn_ln(
        x2d, ffn1_w_in_b, ffn1_w_out_b, ln1_g[i][None], ln1_b[i][None],
        casts=((w_in[i:i + 1], in_scale), (ffn2_w_in[i:i + 1], ffn_scale), (ffn2_w_out[i:i + 1], no_scale)))

    q, k, vt, ygm, ymm = _proj(
        x1, positions.reshape(t // TM_PROJ, 1, TM_PROJ), _rope_freqs(), w_in_b,
        gm_ln_g[i][None], gm_ln_b[i][None], gm_w_s[i], gm_b_s[i].T, km, vmt, seq, n_mem)

    lamv = jnp.stack([lambda_q1[i], lambda_k1[i], lambda_q2[i], lambda_k2[i]]).astype(F32)
    ydf = _diff_attn(lamv, diff_norm_g[i][:, None],
                     q.reshape(bsz, seq, DIFF_Q_WIDTH), k.reshape(bsz, seq, DIFF_Q_WIDTH),
                     vt).reshape(t, DIFF_V_WIDTH)

    x2 = _merge(x1, ygm, ydf, ymm, w_in_b, 0.5 * gate_b[i][None],
                w_branch_gm[i].astype(BF16), w_branch_diff[i].astype(BF16), w_branch_mem[i].astype(BF16),
                (0.5 * w_o[i]).astype(BF16), ln2_g[i][None], ln2_b[i][None])

    x3, = _ffn_ln(x2, ffn2_w_in_b, ffn2_w_out_b, ln3_g[i][None], ln3_b[i][None])
    return x3.reshape(bsz, seq, D_MODEL)
```

```python
import functools
import math

import jax
import jax.numpy as jnp
from jax import lax
from jax.experimental import pallas as pl
from jax.experimental.pallas import tpu as pltpu

F32 = jnp.float32
BF16 = jnp.bfloat16

D_MODEL = 1024
D_FF = 2816
GM_WIDTH = 512
GM_GROUPS = 4
GM_GROUP_DIM = GM_WIDTH // GM_GROUPS
GM_CHUNK = 128
DIFF_HEADS = 4
DIFF_HEAD_DIM = 64
DIFF_V_DIM = 2 * DIFF_HEAD_DIM
DIFF_Q_WIDTH = DIFF_HEADS * 2 * DIFF_HEAD_DIM
DIFF_V_WIDTH = DIFF_HEADS * DIFF_V_DIM
MEM_HEADS = 4
MEM_HEAD_DIM = 64
MEM_WIDTH = MEM_HEADS * MEM_HEAD_DIM
N_BRANCH = 3
ROPE_THETA = 500000.0
ROPE_DIM = DIFF_HEAD_DIM // 4
ROPE_HALF = ROPE_DIM // 2
DEPTH = 1
DEEPNORM_ALPHA = (2 * DEPTH) ** 0.25
LN_EPS = 1e-5
LAM_INIT = 0.8 - 0.6 * math.exp(-0.3 * 0)
OFF_U = 0
OFF_V = OFF_U + GM_WIDTH
OFF_Q = OFF_V + GM_WIDTH
OFF_K = OFF_Q + DIFF_Q_WIDTH
OFF_VD = OFF_K + DIFF_Q_WIDTH
OFF_QM = OFF_VD + DIFF_V_WIDTH
OFF_GATE = OFF_QM + MEM_WIDTH
IN_WIDTH = OFF_GATE + N_BRANCH * D_MODEL

LANES = 128
BF16_SUBLANES = 16
VT_ROWS = DIFF_V_DIM + BF16_SUBLANES
MEM_VT_ROWS = MEM_HEAD_DIM + BF16_SUBLANES
MXU_COLS = 256
VMEM_LIMIT = 52 * 1024 * 1024
ATTN_VMEM_LIMIT = 58 * 1024 * 1024

TM_FFN = 1024
TM_PROJ = 512
TM_MERGE = 1024
TQ = 512

NT_DIMS = (((1,), (1,)), ((), ()))


def _const_spec(shape):
    nd = len(shape)
    return pl.BlockSpec(shape, lambda *_: (0,) * nd, pipeline_mode=pl.Buffered(1))


def _params(n_axes):
    return pltpu.CompilerParams(dimension_semantics=("parallel",) * n_axes,
                                vmem_limit_bytes=VMEM_LIMIT)


def _params_sequential():
    return pltpu.CompilerParams(dimension_semantics=("arbitrary",), vmem_limit_bytes=VMEM_LIMIT)


def _layer_norm(z, g, b):
    mu = jnp.mean(z, axis=-1, keepdims=True)
    zc = z - mu
    var = jnp.mean(zc * zc, axis=-1, keepdims=True)
    return zc * lax.rsqrt(var + LN_EPS) * g + b


def _cast_jobs(casts, n_steps):
    in_specs, args, out_specs, out_shapes = [], [], [], []
    for src, scale in casts:
        _, rows, cols = src.shape
        blk = next(r for r in range(BF16_SUBLANES, rows + 1, BF16_SUBLANES)
                   if rows % r == 0 and rows // r <= n_steps)
        last = rows // blk - 1
        in_specs += [pl.BlockSpec((1, blk, cols), lambda i, last=last: (0, jnp.minimum(i, last), 0)),
                     _const_spec((1, cols))]
        args += [src, scale]
        out_specs.append(pl.BlockSpec((blk, cols), lambda i, last=last: (jnp.minimum(i, last), 0)))
        out_shapes.append(jax.ShapeDtypeStruct((rows, cols), BF16))
    return in_specs, args, out_specs, out_shapes


def _run_casts(cast_in, cast_out):
    for j, out_ref in enumerate(cast_out):
        out_ref[...] = (cast_in[2 * j][0] * cast_in[2 * j + 1][...]).astype(BF16)


def _lagged_row_specs(n_tiles, tm):
    in_spec = lambda width: pl.BlockSpec((tm, width), lambda i: (jnp.minimum(i, n_tiles - 1), 0))
    out_spec = lambda width: pl.BlockSpec((tm, width), lambda i: (jnp.maximum(i - 1, 0), 0))
    return in_spec, out_spec


def _lagged_norm(z_ref, g_ref, b_ref, o_ref):
    n_groups = z_ref.shape[1] // MXU_COLS
    rows = z_ref.shape[0] // n_groups
    zero_rows = []
    for j in range(n_groups):
        rs = slice(j * rows, (j + 1) * rows)
        out = _layer_norm(z_ref[rs, :], g_ref[...], b_ref[...])
        o_ref[rs, :] = out
        bits = lax.bitcast_convert_type(jnp.max(out, axis=0, keepdims=True)[:, :MXU_COLS], jnp.uint32)
        zero_rows.append(lax.bitcast_convert_type((bits >> 16) >> 16, F32))
    return zero_rows


def _run_lagged(tile_fn, z_ref, g_ref, b_ref, o_ref):
    step = pl.program_id(0)
    last = pl.num_programs(0) - 1

    @pl.when(step == 0)
    def _():
        z_ref[...] = jnp.zeros(z_ref.shape, F32)

    @pl.when(step < last)
    def _():
        zero_rows = _lagged_norm(z_ref, g_ref, b_ref, o_ref)
        z_ref[...] = tile_fn(zero_rows)

    @pl.when(step == last)
    def _():
        _lagged_norm(z_ref, g_ref, b_ref, o_ref)


def _ffn_ln_body(n_casts, x_ref, win_ref, wout_ref, g_ref, b_ref, *refs):
    cast_in, o_ref, cast_out = refs[:2 * n_casts], refs[2 * n_casts], refs[2 * n_casts + 1:3 * n_casts + 1]
    z_ref, h_ref = refs[3 * n_casts + 1:]

    def tile(zero_rows):
        _run_casts(cast_in, cast_out)
        x = x_ref[...]
        xb = x.astype(BF16)
        for c in range(D_FF // MXU_COLS):
            cs = slice(c * MXU_COLS, (c + 1) * MXU_COLS)
            g = jnp.dot(xb, win_ref[:, cs], preferred_element_type=F32)
            u = jnp.dot(xb, win_ref[:, D_FF + c * MXU_COLS:D_FF + (c + 1) * MXU_COLS], preferred_element_type=F32)
            if c < len(zero_rows):
                g = g + zero_rows[c]
            h_ref[:, cs] = ((g + g * jnp.tanh(g)) * u).astype(BF16)
        y = jnp.dot(h_ref[...], wout_ref[...], preferred_element_type=F32)
        return DEEPNORM_ALPHA * x + 0.5 * y

    _run_lagged(tile, z_ref, g_ref, b_ref, o_ref)


def _ffn_ln(x2d, w_in, w_out, g, b, casts=()):
    t = x2d.shape[0]
    n_tiles = t // TM_FFN
    in_spec, out_spec = _lagged_row_specs(n_tiles, TM_FFN)
    cast_specs, cast_args, cast_out_specs, cast_out_shapes = _cast_jobs(casts, n_tiles)
    return pl.pallas_call(
        functools.partial(_ffn_ln_body, len(casts)),
        grid=(n_tiles + 1,),
        in_specs=[
            in_spec(D_MODEL),
            _const_spec((D_MODEL, 2 * D_FF)),
            _const_spec((D_FF, D_MODEL)),
            _const_spec((1, D_MODEL)),
            _const_spec((1, D_MODEL)),
        ] + cast_specs,
        out_specs=[out_spec(D_MODEL)] + cast_out_specs,
        out_shape=[jax.ShapeDtypeStruct((t, D_MODEL), F32)] + cast_out_shapes,
        scratch_shapes=[pltpu.VMEM((TM_FFN, D_MODEL), F32), pltpu.VMEM((TM_FFN, D_FF), BF16)],
        compiler_params=_params_sequential(),
        name="ffn_ln",
    )(x2d, w_in, w_out, g, b, *cast_args)


def _mem_kv_body(n_casts, m_ref, w_ref, *refs):
    cast_in, (km_ref, vmt_ref), cast_out = refs[:2 * n_casts], refs[2 * n_casts:2 * n_casts + 2], refs[2 * n_casts + 2:]
    _run_casts(cast_in, cast_out)
    n_mem = m_ref.shape[0]
    kv = jnp.dot(m_ref[...].astype(BF16), w_ref[...], preferred_element_type=F32)
    k = kv[:, :MEM_WIDTH].astype(BF16)
    vt = kv[:, MEM_WIDTH:].T.astype(BF16)
    lane = lax.broadcasted_iota(jnp.int32, k.shape, 1)
    ones = jnp.ones((MEM_VT_ROWS - MEM_HEAD_DIM, n_mem), BF16)
    for hh in range(MEM_HEADS):
        km_ref[0, hh] = jnp.where(lane // MEM_HEAD_DIM == hh, k, jnp.zeros_like(k))
        vmt_ref[0, hh * MEM_VT_ROWS:hh * MEM_VT_ROWS + MEM_HEAD_DIM] = vt[hh * MEM_HEAD_DIM:(hh + 1) * MEM_HEAD_DIM]
        vmt_ref[0, hh * MEM_VT_ROWS + MEM_HEAD_DIM:(hh + 1) * MEM_VT_ROWS] = ones


def _mem_kv(mem2d, w_kv, n_mem, casts=()):
    bsz = mem2d.shape[0] // n_mem
    cast_specs, cast_args, cast_out_specs, cast_out_shapes = _cast_jobs(casts, bsz)
    return pl.pallas_call(
        functools.partial(_mem_kv_body, len(casts)),
        grid=(bsz,),
        in_specs=[
            pl.BlockSpec((n_mem, D_MODEL), lambda i: (i, 0)),
            _const_spec((D_MODEL, 2 * MEM_WIDTH)),
        ] + cast_specs,
        out_specs=[pl.BlockSpec((1, MEM_HEADS, n_mem, MEM_WIDTH), lambda i: (i, 0, 0, 0)),
                   pl.BlockSpec((1, MEM_HEADS * MEM_VT_ROWS, n_mem), lambda i: (i, 0, 0))] + cast_out_specs,
        out_shape=[jax.ShapeDtypeStruct((bsz, MEM_HEADS, n_mem, MEM_WIDTH), BF16),
                   jax.ShapeDtypeStruct((bsz, MEM_HEADS * MEM_VT_ROWS, n_mem), BF16)] + cast_out_shapes,
        compiler_params=_params(1),
        name="mem_kv",
    )(mem2d, w_kv, *cast_args)


def _proj_body(x_ref, pos_ref, invf_ref, w_ref, gmg_ref, gmb_ref, ws_ref, bst_ref, km_ref, vmt_ref,
               q_ref, k_ref, vt_ref, ygm_ref, ymem_ref):
    tm = x_ref.shape[0]
    xb = x_ref[...].astype(BF16)

    def project(off, width):
        return jnp.dot(xb, w_ref[:, off:off + width], preferred_element_type=F32)

    h_qm = project(OFF_QM, MEM_WIDTH)
    h_uv = project(OFF_U, 2 * GM_WIDTH)

    qm = (h_qm * (MEM_HEAD_DIM ** -0.5 * math.log2(math.e))).astype(BF16)
    st = [lax.dot_general(km_ref[0, hh], qm, NT_DIMS, preferred_element_type=F32) for hh in range(MEM_HEADS)]
    h_qk = project(OFF_Q, 2 * DIFF_Q_WIDTH)
    e = [jnp.exp2(s - jnp.max(s, axis=0, keepdims=True)).astype(BF16) for s in st]
    outs = []
    for hh in range(MEM_HEADS):
        ot = jnp.dot(vmt_ref[0, hh * MEM_VT_ROWS:(hh + 1) * MEM_VT_ROWS], e[hh], preferred_element_type=F32)
        outs.append(ot[:MEM_HEAD_DIM] * (1.0 / ot[MEM_HEAD_DIM:MEM_HEAD_DIM + 1]))
    ymem_ref[...] = jnp.concatenate(outs, axis=0).T.astype(BF16)

    u = jax.nn.gelu(h_uv[:, :GM_WIDTH])
    v = jax.nn.gelu(h_uv[:, GM_WIDTH:])
    vn = _layer_norm(v, gmg_ref[...], gmb_ref[...]).astype(BF16)
    row = lax.broadcasted_iota(jnp.int32, (GM_CHUNK, GM_CHUNK), 0)
    col = lax.broadcasted_iota(jnp.int32, (GM_CHUNK, GM_CHUNK), 1)
    causal = col <= row
    h_vd = project(OFF_VD, DIFF_V_WIDTH)
    for g in range(GM_GROUPS):
        wc = jnp.where(causal, ws_ref[g], 0.0).astype(BF16)
        bcol = bst_ref[:, g:g + 1]
        gs = slice(g * GM_GROUP_DIM, (g + 1) * GM_GROUP_DIM)
        for c in range(tm // GM_CHUNK):
            rs = slice(c * GM_CHUNK, (c + 1) * GM_CHUNK)
            mixed = jnp.dot(wc, vn[rs, gs], preferred_element_type=F32) + bcol
            ygm_ref[rs, gs] = (u[rs, gs] * mixed).astype(BF16)

    ang = invf_ref[...] * pos_ref[0].astype(F32)
    cos_t = jnp.cos(ang)
    sin_t = jnp.sin(ang)
    rest = DIFF_HEAD_DIM - ROPE_DIM
    one_t = jnp.ones((rest, tm), F32)
    zero_t = jnp.zeros((rest, tm), F32)
    zero_h = jnp.zeros((ROPE_HALF, tm), F32)
    cos = jnp.concatenate([cos_t, cos_t, one_t] * 2, axis=0).T
    sin_lo = jnp.concatenate([-sin_t, zero_h, zero_t] * 2, axis=0).T
    sin_hi = jnp.concatenate([zero_h, sin_t, zero_t] * 2, axis=0).T

    def rope(t):
        return (t * cos + pltpu.roll(t, LANES - ROPE_HALF, axis=1) * sin_lo
                + pltpu.roll(t, ROPE_HALF, axis=1) * sin_hi)

    scale = DIFF_HEAD_DIM ** -0.5 * math.log2(math.e)
    for j in range(DIFF_Q_WIDTH // LANES):
        cs = slice(j * LANES, (j + 1) * LANES)
        q_ref[:, cs] = (rope(h_qk[:, j * LANES:(j + 1) * LANES]) * scale).astype(BF16)
        k_ref[:, cs] = rope(h_qk[:, DIFF_Q_WIDTH + j * LANES:DIFF_Q_WIDTH + (j + 1) * LANES]).astype(BF16)
    vt = h_vd.T.astype(BF16)
    ones = jnp.ones((VT_ROWS - DIFF_V_DIM, tm), BF16)
    for hh in range(DIFF_HEADS):
        vt_ref[0, hh * VT_ROWS:hh * VT_ROWS + DIFF_V_DIM] = vt[hh * DIFF_V_DIM:(hh + 1) * DIFF_V_DIM]
        vt_ref[0, hh * VT_ROWS + DIFF_V_DIM:(hh + 1) * VT_ROWS] = ones


def _proj(x2d, pos3d, invf, w, gm_g, gm_b, w_s, b_st, km, vmt, seq, n_mem):
    t = x2d.shape[0]
    tiles_per_batch = seq // TM_PROJ
    row_spec = lambda width: pl.BlockSpec((TM_PROJ, width), lambda i: (i, 0))
    return pl.pallas_call(
        _proj_body,
        grid=(t // TM_PROJ,),
        in_specs=[
            row_spec(D_MODEL),
            pl.BlockSpec((1, 1, TM_PROJ), lambda i: (i, 0, 0)),
            _const_spec((ROPE_HALF, 1)),
            _const_spec((D_MODEL, IN_WIDTH)),
            _const_spec((1, GM_WIDTH)),
            _const_spec((1, GM_WIDTH)),
            _const_spec((GM_GROUPS, GM_CHUNK, GM_CHUNK)),
            _const_spec((GM_CHUNK, GM_GROUPS)),
            pl.BlockSpec((1, MEM_HEADS, n_mem, MEM_WIDTH), lambda i: (i // tiles_per_batch, 0, 0, 0)),
            pl.BlockSpec((1, MEM_HEADS * MEM_VT_ROWS, n_mem), lambda i: (i // tiles_per_batch, 0, 0)),
        ],
        out_specs=[row_spec(DIFF_Q_WIDTH), row_spec(DIFF_Q_WIDTH),
                   pl.BlockSpec((1, DIFF_HEADS * VT_ROWS, TM_PROJ),
                                lambda i: (i // tiles_per_batch, 0, i % tiles_per_batch)),
                   row_spec(GM_WIDTH), row_spec(MEM_WIDTH)],
        out_shape=[jax.ShapeDtypeStruct((t, DIFF_Q_WIDTH), BF16),
                   jax.ShapeDtypeStruct((t, DIFF_Q_WIDTH), BF16),
                   jax.ShapeDtypeStruct((t // seq, DIFF_HEADS * VT_ROWS, seq), BF16),
                   jax.ShapeDtypeStruct((t, GM_WIDTH), BF16),
                   jax.ShapeDtypeStruct((t, MEM_WIDTH), BF16)],
        compiler_params=_params(1),
        name="proj",
    )(x2d, pos3d, invf, w, gm_g, gm_b, w_s, b_st, km, vmt)


def _diff_attn_body(lamv_ref, ngc_ref, q_ref, k_ref, vt_ref, o_ref, qs_ref, st_ref, e_ref, acc_ref):
    heads = range(DIFF_HEADS)
    lane = lax.broadcasted_iota(jnp.int32, (TQ, DIFF_V_DIM), 1)
    kpos = lax.broadcasted_iota(jnp.int32, (TQ, 2 * TQ), 0)
    qpos = lax.broadcasted_iota(jnp.int32, (TQ, 2 * TQ), 1) % TQ
    visible = kpos <= qpos
    lv = lamv_ref[...]
    lam = (jnp.exp(jnp.sum(lv[0:1] * lv[1:2], axis=-1, keepdims=True))
           - jnp.exp(jnp.sum(lv[2:3] * lv[3:4], axis=-1, keepdims=True)) + LAM_INIT)

    def values(h, ki, drop=None):
        vtb = vt_ref[0, h * VT_ROWS:(h + 1) * VT_ROWS, pl.ds(pl.multiple_of(ki * TQ, TQ), TQ)]
        if drop is not None:
            vtb = jnp.where(drop, jnp.zeros_like(vtb), vtb)
        return jnp.dot(vtb, e_ref[h], preferred_element_type=F32)

    def softmax(h, st, m, bm):
        m_new = jnp.maximum(m, bm)
        e_ref[h] = jnp.exp2(st - m_new).astype(BF16)
        return jnp.exp2(m - m_new), m_new

    class QBlock:
        def __init__(self, s):
            self.s = s
            self.qi = 2 * pl.program_id(1) + s
            self.rows = slice(s * TQ, (s + 1) * TQ)

        def scores(self, h, ki, clear_probs=False, clear_acc=False):
            kb = k_ref[0, pl.ds(pl.multiple_of(ki * TQ, TQ), TQ), h * DIFF_V_DIM:(h + 1) * DIFF_V_DIM]
            st = lax.dot_general(kb, qs_ref[self.s, h], NT_DIMS, preferred_element_type=F32)
            st_ref[self.s, h] = st
            if clear_probs or clear_acc:
                zero = lax.bitcast_convert_type((lax.bitcast_convert_type(st, jnp.uint32) >> 16) >> 16, F32)
                if clear_probs:
                    e_ref[h] = zero.astype(BF16)
                if clear_acc:
                    acc_ref[self.s, h] = zero[:VT_ROWS]
            return jnp.max(st, axis=0, keepdims=True)

        def start(self, clear_probs):
            for h in heads:
                q = q_ref[0, self.rows, h * DIFF_V_DIM:(h + 1) * DIFF_V_DIM]
                zero = jnp.zeros_like(q)
                qs_ref[self.s, h, :TQ] = jnp.where(lane < DIFF_HEAD_DIM, q, zero)
                qs_ref[self.s, h, TQ:] = jnp.where(lane >= DIFF_HEAD_DIM, q, zero)
            self.init = (tuple(jnp.ones((1, 2 * TQ), F32) for _ in heads),
                         tuple(jnp.full((1, 2 * TQ), -jnp.inf, F32) for _ in heads),
                         tuple(self.scores(h, 0, clear_probs, True) for h in heads))

        def loop(self, stale_probs):
            def step(t, carry):
                a_prev, m, bm = carry
                drop = (t == 0) if stale_probs else None
                pv = [values(h, jnp.maximum(t - 1, 0), drop) for h in heads]
                am = [softmax(h, st_ref[self.s, h], m[h], bm[h]) for h in heads]
                for h in heads:
                    acc_ref[self.s, h] = a_prev[h] * acc_ref[self.s, h] + pv[h]
                return (tuple(x[0] for x in am), tuple(x[1] for x in am),
                        tuple(self.scores(h, t + 1) for h in heads))

            self.a_prev, self.m, _ = lax.fori_loop(0, self.qi, step, self.init)

        def last_values(self):
            self.pv = [values(h, jnp.maximum(self.qi - 1, 0)) for h in heads]

        def finish(self):
            for h in heads:
                st = jnp.where(visible, st_ref[self.s, h], -jnp.inf)
                a, _ = softmax(h, st, self.m[h], jnp.max(st, axis=0, keepdims=True))
                acc = a * (self.a_prev[h] * acc_ref[self.s, h] + self.pv[h]) + values(h, self.qi)
                on = acc[:DIFF_V_DIM] * (1.0 / acc[DIFF_V_DIM:DIFF_V_DIM + 1])
                ot = on[:, :TQ] - lam * on[:, TQ:]
                ot = ot * lax.rsqrt(jnp.mean(ot * ot, axis=0, keepdims=True) + LN_EPS) * ngc_ref[...]
                o_ref[0, self.rows, h * DIFF_V_DIM:(h + 1) * DIFF_V_DIM] = (ot * (1.0 - LAM_INIT)).T.astype(BF16)

    first, second = QBlock(0), QBlock(1)
    first.start(clear_probs=True)
    first.loop(stale_probs=False)
    first.last_values()
    second.start(clear_probs=False)
    first.finish()
    second.loop(stale_probs=True)
    second.last_values()
    second.finish()


def _diff_attn(lamv, norm_g_col, q, k, vt):
    b, s, _ = q.shape
    return pl.pallas_call(
        _diff_attn_body,
        grid=(b, s // (2 * TQ)),
        in_specs=[
            _const_spec((4, DIFF_HEAD_DIM)),
            _const_spec((DIFF_V_DIM, 1)),
            pl.BlockSpec((1, 2 * TQ, DIFF_Q_WIDTH), lambda bi, qi: (bi, qi, 0)),
            pl.BlockSpec((1, s, DIFF_Q_WIDTH), lambda bi, qi: (bi, 0, 0)),
            pl.BlockSpec((1, DIFF_HEADS * VT_ROWS, s), lambda bi, qi: (bi, 0, 0)),
        ],
        out_specs=pl.BlockSpec((1, 2 * TQ, DIFF_V_WIDTH), lambda bi, qi: (bi, qi, 0)),
        out_shape=jax.ShapeDtypeStruct((b, s, DIFF_V_WIDTH), BF16),
        scratch_shapes=[pltpu.VMEM((2, DIFF_HEADS, 2 * TQ, DIFF_V_DIM), BF16),
                        pltpu.VMEM((2, DIFF_HEADS, TQ, 2 * TQ), F32),
                        pltpu.VMEM((DIFF_HEADS, TQ, 2 * TQ), BF16),
                        pltpu.VMEM((2, DIFF_HEADS, VT_ROWS, 2 * TQ), F32)],
        compiler_params=pltpu.CompilerParams(dimension_semantics=("parallel", "parallel"),
                                             vmem_limit_bytes=ATTN_VMEM_LIMIT),
        name="diff_attn",
    )(lamv, norm_g_col, q, k, vt)


def _merge_body(x_ref, ygm_ref, ydf_ref, ymm_ref, wgate_ref, gb_ref, wgm_ref, wdf_ref, wmm_ref, wo_ref,
                g_ref, b_ref, o_ref, z_ref, m_ref):
    def tile(zero_rows):
        x = x_ref[...]
        xb = x.astype(BF16)
        branch_refs = ((ygm_ref, wgm_ref), (ydf_ref, wdf_ref), (ymm_ref, wmm_ref))
        for c in range(D_MODEL // MXU_COLS):
            cs = slice(c * MXU_COLS, (c + 1) * MXU_COLS)
            merged = zero_rows[c]
            for r, (y_ref, w_ref) in enumerate(branch_refs):
                gs = slice(r * D_MODEL + c * MXU_COLS, r * D_MODEL + (c + 1) * MXU_COLS)
                y = jnp.dot(y_ref[...], w_ref[:, cs], preferred_element_type=F32)
                logits = jnp.dot(xb, wgate_ref[:, OFF_GATE + gs.start:OFF_GATE + gs.stop],
                                 preferred_element_type=F32)
                merged = merged + (y + jnp.tanh(logits + gb_ref[:, gs]) * y)
            m_ref[:, cs] = merged.astype(BF16)
        y = jnp.dot(m_ref[...], wo_ref[...], preferred_element_type=F32)
        return DEEPNORM_ALPHA * x + y

    _run_lagged(tile, z_ref, g_ref, b_ref, o_ref)


def _merge(x2d, ygm, ydf, ymm, w_gate, gate_b, w_gm, w_df, w_mm, w_o, g, b):
    t = x2d.shape[0]
    n_tiles = t // TM_MERGE
    row_spec, out_spec = _lagged_row_specs(n_tiles, TM_MERGE)
    return pl.pallas_call(
        _merge_body,
        grid=(n_tiles + 1,),
        in_specs=[
            row_spec(D_MODEL), row_spec(GM_WIDTH), row_spec(DIFF_V_WIDTH), row_spec(MEM_WIDTH),
            _const_spec((D_MODEL, IN_WIDTH)),
            _const_spec((1, N_BRANCH * D_MODEL)),
            _const_spec((GM_WIDTH, D_MODEL)),
            _const_spec((DIFF_V_WIDTH, D_MODEL)),
            _const_spec((MEM_WIDTH, D_MODEL)),
            _const_spec((D_MODEL, D_MODEL)),
            _const_spec((1, D_MODEL)),
            _const_spec((1, D_MODEL)),
        ],
        out_specs=out_spec(D_MODEL),
        out_shape=jax.ShapeDtypeStruct((t, D_MODEL), F32),
        scratch_shapes=[pltpu.VMEM((TM_MERGE, D_MODEL), F32), pltpu.VMEM((TM_MERGE, D_MODEL), BF16)],
        compiler_params=_params_sequential(),
        name="merge",
    )(x2d, ygm, ydf, ymm, w_gate, gate_b, w_gm, w_df, w_mm, w_o, g, b)


def _rope_freqs():
    return (ROPE_THETA ** (-jnp.arange(0, ROPE_DIM, 2, dtype=F32) / ROPE_DIM)).reshape(ROPE_HALF, 1)


def kernel(x, mem, positions, ffn1_w_in, ffn1_w_out, ln1_g, ln1_b, w_in, gate_b, gm_ln_g, gm_ln_b, gm_w_s, gm_b_s,
           lambda_q1, lambda_k1, lambda_q2, lambda_k2, diff_norm_g, w_mem_kv, w_branch_gm, w_branch_diff,
           w_branch_mem, w_o, ln2_g, ln2_b, ffn2_w_in, ffn2_w_out, ln3_g, ln3_b):
    bsz, seq, _ = x.shape
    n_mem = mem.shape[1]
    t = bsz * seq
    i = 0
    x2d = x.reshape(t, D_MODEL)

    ffn_scale = jnp.concatenate([jnp.full((1, D_FF), 0.5, F32), jnp.ones((1, D_FF), F32)], axis=1)
    in_scale = jnp.concatenate([jnp.ones((1, OFF_GATE), F32), jnp.full((1, N_BRANCH * D_MODEL), 0.5, F32)], axis=1)

    no_scale = jnp.ones((1, D_MODEL), F32)
    km, vmt, ffn1_w_in_b, ffn1_w_out_b = _mem_kv(
        mem.reshape(bsz * n_mem, D_MODEL), w_mem_kv[i].astype(BF16), n_mem,
        casts=((ffn1_w_in[i:i + 1], ffn_scale), (ffn1_w_out[i:i + 1], no_scale)))
    x1, w_in_b, ffn2_w_in_b, ffn2_w_out_b = _ffn_ln(
        x2d, ffn1_w_in_b, ffn1_w_out_b, ln1_g[i][None], ln1_b[i][None],
        casts=((w_in[i:i + 1], in_scale), (ffn2_w_in[i:i + 1], ffn_scale), (ffn2_w_out[i:i + 1], no_scale)))

    q, k, vt, ygm, ymm = _proj(
        x1, positions.reshape(t // TM_PROJ, 1, TM_PROJ), _rope_freqs(), w_in_b,
        gm_ln_g[i][None], gm_ln_b[i][None], gm_w_s[i], gm_b_s[i].T, km, vmt, seq, n_mem)

    lamv = jnp.stack([lambda_q1[i], lambda_k1[i], lambda_q2[i], lambda_k2[i]]).astype(F32)
    ydf = _diff_attn(lamv, diff_norm_g[i][:, None],
                     q.reshape(bsz, seq, DIFF_Q_WIDTH), k.reshape(bsz, seq, DIFF_Q_WIDTH),
                     vt).reshape(t, DIFF_V_WIDTH)

    x2 = _merge(x1, ygm, ydf, ymm, w_in_b, 0.5 * gate_b[i][None],
                w_branch_gm[i].astype(BF16), w_branch_diff[i].astype(BF16), w_branch_mem[i].astype(BF16),
                (0.5 * w_o[i]).astype(BF16), ln2_g[i][None], ln2_b[i][None])

    x3, = _ffn_ln(x2, ffn2_w_in_b, ffn2_w_out_b, ln3_g[i][None], ln3_b[i][None])
    return x3.reshape(bsz, seq, D_MODEL)
```
